```python
import math
import jax, jax.numpy as jnp
from jax import lax
import numpy as np

D_MODEL = 4096
BATCH = 4
SEQ = 2048
DEPTH = 2
DEC_BATCH = 8
DEC_SEQ = 8
PAST_LEN = 16384
PAGE_SIZE = 128

MIX_WIDTH = D_MODEL
SSM_WIDTH = MIX_WIDTH // 2
ATT_WIDTH = MIX_WIDTH - SSM_WIDTH
SSM_GROUP = 16
N_SSM_GROUPS = SSM_WIDTH // SSM_GROUP
SSM_STATE = 64
HEAD_DIM = 128
N_HEADS = ATT_WIDTH // HEAD_DIM
IN_WIDTH = SSM_WIDTH + 3 * ATT_WIDTH
D_FF = 4 * D_MODEL
Q_BLOCK = 128
RMS_EPS = 1e-6
DT_MIN = 1e-3
DT_MAX = 1e-1
SB_BIAS_INIT = -8.0

kernel_name = "hymba_s5_stickbreaking_step"


def rms_norm(x, g):
    xf = x.astype(jnp.float32)
    y = xf * lax.rsqrt(jnp.mean(xf * xf, axis=-1, keepdims=True) + RMS_EPS)
    return (y * g.astype(jnp.float32)).astype(x.dtype)


def s5_mixer(u, h0_re, h0_im, lam_re, lam_im, log_dt, b_re, b_im, c_re, c_im, d_skip, w_glu, b_glu):
    bsz, t_len, _ = u.shape
    uf = u.astype(jnp.float32).reshape(bsz, t_len, N_SSM_GROUPS, SSM_GROUP)
    lr = lam_re.astype(jnp.float32)
    li = lam_im.astype(jnp.float32)
    dt = jnp.exp(log_dt.astype(jnp.float32))[:, None]
    mag = jnp.exp(lr * dt)
    ab_re = mag * jnp.cos(li * dt)
    ab_im = mag * jnp.sin(li * dt)
    w_re = ab_re - 1.0
    w_im = ab_im
    den = lr * lr + li * li
    coef_re = (w_re * lr + w_im * li) / den
    coef_im = (w_im * lr - w_re * li) / den
    br = b_re.astype(jnp.float32)
    bi = b_im.astype(jnp.float32)
    bb_re = coef_re[..., None] * br - coef_im[..., None] * bi
    bb_im = coef_re[..., None] * bi + coef_im[..., None] * br
    x_re = jnp.einsum('gnp,btgp->btgn', bb_re, uf)
    x_im = jnp.einsum('gnp,btgp->btgn', bb_im, uf)
    a_re = jnp.broadcast_to(ab_re, (1, t_len, N_SSM_GROUPS, SSM_STATE))
    a_im = jnp.broadcast_to(ab_im, (1, t_len, N_SSM_GROUPS, SSM_STATE))

    def combine(e1, e2):
        a1r, a1i, b1r, b1i = e1
        a2r, a2i, b2r, b2i = e2
        return (a2r * a1r - a2i * a1i,
                a2r * a1i + a2i * a1r,
                a2r * b1r - a2i * b1i + b2r,
                a2r * b1i + a2i * b1r + b2i)

    acum_re, acum_im, hz_re, hz_im = lax.associative_scan(combine, (a_re, a_im, x_re, x_im), axis=1)
    h0r = h0_re.astype(jnp.float32)[:, None]
    h0i = h0_im.astype(jnp.float32)[:, None]
    h_re = hz_re + acum_re * h0r - acum_im * h0i
    h_im = hz_im + acum_re * h0i + acum_im * h0r
    y = (jnp.einsum('gpn,btgn->btgp', c_re.astype(jnp.float32), h_re)
         - jnp.einsum('gpn,btgn->btgp', c_im.astype(jnp.float32), h_im)
         + d_skip.astype(jnp.float32) * uf)
    y = jax.nn.gelu(y.reshape(bsz, t_len, SSM_WIDTH))
    out = y * jax.nn.sigmoid(y @ w_glu.astype(jnp.float32) + b_glu.astype(jnp.float32))
    return out.astype(u.dtype), h_re[:, -1], h_im[:, -1]


def stick_breaking_attention(q, k, v, sb_bias, q_pos, k_pos):
    bsz, t_q = q.shape[:2]
    qb = Q_BLOCK if t_q % Q_BLOCK == 0 else t_q
    nb = t_q // qb
    q_blocks = jnp.moveaxis(q.reshape(bsz, nb, qb, N_HEADS, HEAD_DIM), 1, 0)
    pos_blocks = q_pos.reshape(nb, qb)
    scale = HEAD_DIM ** -0.5
    bias = sb_bias.astype(jnp.float32)[None, :, None, None]

    def one_block(args):
        q_blk, qp = args
        z = jnp.einsum('bqhd,bkhd->bhqk', q_blk, k, preferred_element_type=jnp.float32) * scale + bias
        causal = k_pos[None, :] < qp[:, None]
        log_keep = jnp.where(causal, jax.nn.log_sigmoid(-z), 0.0)
        log_after = lax.cumsum(log_keep, axis=3, reverse=True) - log_keep
        w = jnp.where(causal, jnp.exp(jax.nn.log_sigmoid(z) + log_after), 0.0)
        return jnp.einsum('bhqk,bkhd->bqhd', w, v.astype(jnp.float32))

    out = lax.map(one_block, (q_blocks, pos_blocks))
    return jnp.moveaxis(out, 0, 1).reshape(bsz, t_q, ATT_WIDTH).astype(q.dtype)


def hybrid_layer(x, h0_re, h0_im, k_past, v_past, p):
    bsz, t_len, _ = x.shape
    past = 0 if k_past is None else k_past.shape[1]
    hn = rms_norm(x, p['norm_mix_g'])
    proj = hn @ p['w_in']
    u, q, k, v = jnp.split(proj, [SSM_WIDTH, SSM_WIDTH + ATT_WIDTH, SSM_WIDTH + 2 * ATT_WIDTH], axis=-1)
    q = q.reshape(bsz, t_len, N_HEADS, HEAD_DIM)
    k = k.reshape(bsz, t_len, N_HEADS, HEAD_DIM)
    v = v.reshape(bsz, t_len, N_HEADS, HEAD_DIM)
    ssm_out, h_re, h_im = s5_mixer(u, h0_re, h0_im, p['lam_re'], p['lam_im'], p['log_dt'],
                                   p['b_re'], p['b_im'], p['c_re'], p['c_im'], p['d_skip'],
                                   p['w_glu'], p['b_glu'])
    k_all = k if k_past is None else jnp.concatenate([k_past.astype(k.dtype), k], axis=1)
    v_all = v if v_past is None else jnp.concatenate([v_past.astype(v.dtype), v], axis=1)
    q_pos = past + jnp.arange(t_len, dtype=jnp.int32)
    k_pos = jnp.arange(past + t_len, dtype=jnp.int32)
    att_out = stick_breaking_attention(q, k_all, v_all, p['sb_bias'], q_pos, k_pos)
    mixed = jnp.concatenate([rms_norm(ssm_out, p['g_ssm']), rms_norm(att_out, p['g_att'])], axis=-1)
    x = x + mixed @ p['w_out']
    hf = rms_norm(x, p['norm_ffn_g'])
    x = x + jnp.square(jax.nn.relu(hf @ p['w_ff1'])) @ p['w_ff2']
    return x, k, v, h_re, h_im


def setup_inputs(seed: int = 0) -> dict:
    key = jax.random.key(seed)
    ks = jax.random.split(key, 32)
    n_pages = PAST_LEN // PAGE_SIZE
    n_used = DEC_BATCH * n_pages
    n_phys = (n_used * 5) // 4
    f32 = jnp.float32
    nrm = lambda k, shape, s: jax.random.normal(k, shape, f32) * s
    x_prompt = nrm(ks[0], (BATCH, SEQ, D_MODEL), 1.0)
    x_sample = nrm(ks[1], (DEC_BATCH, DEC_SEQ, D_MODEL), 1.0)
    cache_k = nrm(ks[2], (DEPTH, n_phys, PAGE_SIZE, N_HEADS, HEAD_DIM), 1.0)
    cache_v = nrm(ks[3], (DEPTH, n_phys, PAGE_SIZE, N_HEADS, HEAD_DIM), 1.0)
    state_ssm_re = nrm(ks[4], (DEPTH, DEC_BATCH, N_SSM_GROUPS, SSM_STATE), 0.5)
    state_ssm_im = nrm(ks[5], (DEPTH, DEC_BATCH, N_SSM_GROUPS, SSM_STATE), 0.5)
    page_table = jax.random.permutation(ks[6], n_phys)[:n_used].reshape(DEC_BATCH, n_pages).astype(jnp.int32)
    norm_mix_g = 1.0 + nrm(ks[7], (DEPTH, D_MODEL), 0.02)
    w_in = nrm(ks[8], (DEPTH, D_MODEL, IN_WIDTH), D_MODEL ** -0.5)
    n_idx = jnp.arange(SSM_STATE, dtype=f32)
    lam_re = -0.5 + nrm(ks[9], (DEPTH, N_SSM_GROUPS, SSM_STATE), 0.01)
    lam_im = math.pi * n_idx + nrm(ks[10], (DEPTH, N_SSM_GROUPS, SSM_STATE), 0.01)
    log_dt = jax.random.uniform(ks[11], (DEPTH, N_SSM_GROUPS), f32, math.log(DT_MIN), math.log(DT_MAX))
    b_re = nrm(ks[12], (DEPTH, N_SSM_GROUPS, SSM_STATE, SSM_GROUP), (2 * SSM_GROUP) ** -0.5)
    b_im = nrm(ks[13], (DEPTH, N_SSM_GROUPS, SSM_STATE, SSM_GROUP), (2 * SSM_GROUP) ** -0.5)
    c_re = nrm(ks[14], (DEPTH, N_SSM_GROUPS, SSM_GROUP, SSM_STATE), (2 * SSM_STATE) ** -0.5)
    c_im = nrm(ks[15], (DEPTH, N_SSM_GROUPS, SSM_GROUP, SSM_STATE), (2 * SSM_STATE) ** -0.5)
    d_skip = nrm(ks[16], (DEPTH, N_SSM_GROUPS, SSM_GROUP), 1.0)
    w_glu = nrm(ks[17], (DEPTH, SSM_WIDTH, SSM_WIDTH), SSM_WIDTH ** -0.5)
    b_glu = nrm(ks[18], (DEPTH, SSM_WIDTH), 0.02)
    sb_bias = SB_BIAS_INIT + nrm(ks[26], (DEPTH, N_HEADS), 0.1)
    g_ssm = 1.0 + nrm(ks[19], (DEPTH, SSM_WIDTH), 0.02)
    g_att = 1.0 + nrm(ks[20], (DEPTH, ATT_WIDTH), 0.02)
    w_out = nrm(ks[21], (DEPTH, MIX_WIDTH, D_MODEL), MIX_WIDTH ** -0.5)
    norm_ffn_g = 1.0 + nrm(ks[22], (DEPTH, D_MODEL), 0.02)
    w_ff1 = nrm(ks[23], (DEPTH, D_MODEL, D_FF), D_MODEL ** -0.5)
    w_ff2 = nrm(ks[24], (DEPTH, D_FF, D_MODEL), D_FF ** -0.5)
    final_g = 1.0 + nrm(ks[25], (D_MODEL,), 0.02)
    return {"x_prompt": x_prompt, "x_sample": x_sample,
            "cache_k": cache_k, "cache_v": cache_v,
            "state_ssm_re": state_ssm_re, "state_ssm_im": state_ssm_im,
            "page_table": page_table,
            "norm_mix_g": norm_mix_g, "w_in": w_in,
            "lam_re": lam_re, "lam_im": lam_im, "log_dt": log_dt,
            "b_re": b_re, "b_im": b_im, "c_re": c_re, "c_im": c_im, "d_skip": d_skip,
            "w_glu": w_glu, "b_glu": b_glu, "sb_bias": sb_bias, "g_ssm": g_ssm, "g_att": g_att,
            "w_out": w_out, "norm_ffn_g": norm_ffn_g, "w_ff1": w_ff1, "w_ff2": w_ff2,
            "final_g": final_g}


def reference(x_prompt, x_sample, cache_k, cache_v, state_ssm_re, state_ssm_im, page_table,
              norm_mix_g, w_in, lam_re, lam_im, log_dt, b_re, b_im, c_re, c_im, d_skip,
              w_glu, b_glu, sb_bias, g_ssm, g_att, w_out, norm_ffn_g, w_ff1, w_ff2, final_g):
    n_pages = PAST_LEN // PAGE_SIZE
    b_prompt = x_prompt.shape[0]
    b_dec = x_sample.shape[0]
    yp, ys = x_prompt, x_sample
    kp_l, vp_l, hpr_l, hpi_l = [], [], [], []
    ks_l, vs_l, hsr_l, hsi_l = [], [], [], []
    h_zero = jnp.zeros((b_prompt, N_SSM_GROUPS, SSM_STATE), jnp.float32)
    for l in range(DEPTH):
        p = {'norm_mix_g': norm_mix_g[l], 'w_in': w_in[l],
             'lam_re': lam_re[l], 'lam_im': lam_im[l], 'log_dt': log_dt[l],
             'b_re': b_re[l], 'b_im': b_im[l], 'c_re': c_re[l], 'c_im': c_im[l],
             'd_skip': d_skip[l], 'w_glu': w_glu[l], 'b_glu': b_glu[l], 'sb_bias': sb_bias[l],
             'g_ssm': g_ssm[l], 'g_att': g_att[l], 'w_out': w_out[l],
             'norm_ffn_g': norm_ffn_g[l], 'w_ff1': w_ff1[l], 'w_ff2': w_ff2[l]}
        yp, kp, vp, hpr, hpi = hybrid_layer(yp, h_zero, h_zero, None, None, p)
        k_past = cache_k[l][page_table].reshape(b_dec, n_pages * PAGE_SIZE, N_HEADS, HEAD_DIM)
        v_past = cache_v[l][page_table].reshape(b_dec, n_pages * PAGE_SIZE, N_HEADS, HEAD_DIM)
        ys, ksm, vsm, hsr, hsi = hybrid_layer(ys, state_ssm_re[l], state_ssm_im[l], k_past, v_past, p)
        kp_l.append(kp); vp_l.append(vp); hpr_l.append(hpr); hpi_l.append(hpi)
        ks_l.append(ksm); vs_l.append(vsm); hsr_l.append(hsr); hsi_l.append(hsi)
    y_prompt = rms_norm(yp, final_g)
    y_sample = rms_norm(ys, final_g)
    new_k_prompt = jnp.stack(kp_l)
    new_v_prompt = jnp.stack(vp_l)
    new_ssm_re_prompt = jnp.stack(hpr_l)
    new_ssm_im_prompt = jnp.stack(hpi_l)
    new_k_sample = jnp.stack(ks_l)
    new_v_sample = jnp.stack(vs_l)
    new_ssm_re_sample = jnp.stack(hsr_l)
    new_ssm_im_sample = jnp.stack(hsi_l)
    return (y_prompt, y_sample, new_k_prompt, new_v_prompt, new_ssm_re_prompt, new_ssm_im_prompt,
            new_k_sample, new_v_sample, new_ssm_re_sample, new_ssm_im_sample)
```

```python
import functools
import math

import jax
import jax.numpy as jnp
from jax import lax
from jax.experimental import pallas as pl
from jax.experimental.pallas import tpu as pltpu

F32 = jnp.float32
BF16 = jnp.bfloat16

LANES = 128
SUBLANES = 8
VMEM_LIMIT = 56 * 1024 * 1024

RMS_EPS = 1e-6
HEAD_DIM = 128
SSM_GROUP = 16
SSM_STATE = 64
PAGE_SIZE = 128
GROUPS_PER_BLOCK = LANES // SSM_GROUP
STATES_PER_BLOCK = GROUPS_PER_BLOCK * SSM_STATE
STATE_COLS = STATES_PER_BLOCK // LANES


def _params(*sem):
    return pltpu.CompilerParams(dimension_semantics=sem, vmem_limit_bytes=VMEM_LIMIT)


def _rms_scale(x):
    return x * lax.rsqrt(jnp.mean(x * x, axis=-1, keepdims=True) + RMS_EPS)


def _rmsnorm_kernel(x_ref, g_ref, o_ref):
    o_ref[...] = (_rms_scale(x_ref[...]) * g_ref[...]).astype(o_ref.dtype)


def _rmsnorm(x, g, out_dtype, rows_per_step=256):
    rows, d = x.shape
    tr = min(rows_per_step, rows)
    return pl.pallas_call(
        _rmsnorm_kernel,
        grid=(rows // tr,),
        in_specs=[pl.BlockSpec((tr, d), lambda i: (i, 0)),
                  pl.BlockSpec((1, d), lambda i: (0, 0))],
        out_specs=pl.BlockSpec((tr, d), lambda i: (i, 0)),
        out_shape=jax.ShapeDtypeStruct((rows, d), out_dtype),
        compiler_params=_params("arbitrary"),
        name="rmsnorm",
    )(x, g.reshape(1, d))


def _rmsnorm_pair_kernel(a_ref, ga_ref, b_ref, gb_ref, o_ref):
    wa = a_ref.shape[-1]
    o_ref[:, :wa] = (_rms_scale(a_ref[...]) * ga_ref[...]).astype(o_ref.dtype)
    o_ref[:, wa:] = (_rms_scale(b_ref[...]) * gb_ref[...]).astype(o_ref.dtype)


def _rmsnorm_pair(a, ga, b, gb, rows_per_step=256):
    rows, wa = a.shape
    wb = b.shape[1]
    tr = min(rows_per_step, rows)
    return pl.pallas_call(
        _rmsnorm_pair_kernel,
        grid=(rows // tr,),
        in_specs=[pl.BlockSpec((tr, wa), lambda i: (i, 0)),
                  pl.BlockSpec((1, wa), lambda i: (0, 0)),
                  pl.BlockSpec((tr, wb), lambda i: (i, 0)),
                  pl.BlockSpec((1, wb), lambda i: (0, 0))],
        out_specs=pl.BlockSpec((tr, wa + wb), lambda i: (i, 0)),
        out_shape=jax.ShapeDtypeStruct((rows, wa + wb), BF16),
        compiler_params=_params("arbitrary"),
        name="rmsnorm_pair",
    )(a, ga.reshape(1, wa), b, gb.reshape(1, wb))


def _mm_kernel(*refs, nk, epilogue):
    n_extra = {"none": 0, "relu2": 0, "residual": 1, "glu": 2}[epilogue]
    a_ref, w_ref = refs[0], refs[1]
    extra = refs[2:2 + n_extra]
    o_ref = refs[2 + n_extra]

    def finish(acc):
        if epilogue == "residual":
            acc = extra[0][...] + acc
        elif epilogue == "relu2":
            acc = jnp.square(jnp.maximum(acc, 0.0))
        elif epilogue == "glu":
            y = extra[0][...]
            acc = y * jax.nn.sigmoid(acc + extra[1][...])
        o_ref[...] = acc.astype(o_ref.dtype)

    part = jnp.dot(a_ref[...].astype(BF16), w_ref[...].astype(BF16),
                   preferred_element_type=F32)
    if nk == 1:
        finish(part)
        return
    acc_ref = refs[3 + n_extra]
    k = pl.program_id(2)

    @pl.when(k == 0)
    def _():
        acc_ref[...] = part

    @pl.when(k > 0)
    def _():
        acc_ref[...] += part

    @pl.when(k == nk - 1)
    def _():
        finish(acc_ref[...])


def _matmul(a, w, *, col_start=0, n_out=None, epilogue="none", extras=(),
            out_dtype=F32, tm=1024, tn=512, tk=None):
    m, kdim = a.shape
    n_out = w.shape[1] if n_out is None else n_out
    tm = min(tm, m)
    tn = min(tn, n_out)
    tk = kdim if tk is None else min(tk, kdim)
    assert m % tm == 0 and n_out % tn == 0 and kdim % tk == 0 and col_start % tn == 0
    nk = kdim // tk
    col_blk = col_start // tn
    in_specs = [pl.BlockSpec((tm, tk), lambda i, j, k: (i, k)),
                pl.BlockSpec((tk, tn), lambda i, j, k: (k, j + col_blk))]
    for e in extras:
        if e.shape[0] == 1:
            in_specs.append(pl.BlockSpec((1, tn), lambda i, j, k: (0, j)))
        else:
            in_specs.append(pl.BlockSpec((tm, tn), lambda i, j, k: (i, j)))
    scratch = [pltpu.VMEM((tm, tn), F32)] if nk > 1 else []
    return pl.pallas_call(
        functools.partial(_mm_kernel, nk=nk, epilogue=epilogue),
        grid=(m // tm, n_out // tn, nk),
        in_specs=in_specs,
        out_specs=pl.BlockSpec((tm, tn), lambda i, j, k: (i, j)),
        out_shape=jax.ShapeDtypeStruct((m, n_out), out_dtype),
        scratch_shapes=scratch,
        compiler_params=_params("arbitrary", "arbitrary", "arbitrary"),
        name="matmul_" + epilogue,
    )(a, w, *extras)


def _s5_kernel(u_ref, h0r_ref, h0i_ref, ar_ref, ai_ref, bblk_ref, cblk_ref, d_ref,
               y_ref, hr_out, hi_out, slab_ref, h_ref, *, pitch):
    nb, tt_len, _ = u_ref.shape
    nc = STATE_COLS
    tt = pl.program_id(1)

    @pl.when(tt == 0)
    def _():
        for j in range(nc):
            h_ref[j] = h0r_ref[:, j * LANES:(j + 1) * LANES]
            h_ref[nc + j] = h0i_ref[:, j * LANES:(j + 1) * LANES]

    bb = bblk_ref[...]
    for b in range(nb):
        x = jnp.dot(u_ref[b].astype(BF16), bb, preferred_element_type=F32)
        for j in range(2 * nc):
            slab_ref[j, b * pitch:b * pitch + tt_len, :] = x[:, j * LANES:(j + 1) * LANES]

    a_re = [jnp.broadcast_to(ar_ref[:, j * LANES:(j + 1) * LANES], (nb, LANES)) for j in range(nc)]
    a_im = [jnp.broadcast_to(ai_ref[:, j * LANES:(j + 1) * LANES], (nb, LANES)) for j in range(nc)]

    def step(t, carry):
        new = []
        for j in range(nc):
            h_re, h_im = carry[j], carry[nc + j]
            rows = pl.ds(t, nb, stride=pitch)
            re_slab, im_slab = slab_ref.at[j], slab_ref.at[nc + j]
            n_re = a_re[j] * h_re - a_im[j] * h_im + re_slab[rows, :]
            n_im = a_re[j] * h_im + a_im[j] * h_re + im_slab[rows, :]
            re_slab[rows, :] = n_re
            im_slab[rows, :] = n_im
            new.append((n_re, n_im))
        return tuple(p[0] for p in new) + tuple(p[1] for p in new)

    h_last = lax.fori_loop(0, tt_len, step, tuple(h_ref[j] for j in range(2 * nc)),
                           unroll=min(4, tt_len))
    for j in range(2 * nc):
        h_ref[j] = h_last[j]

    cb = cblk_ref[...]
    for b in range(nb):
        h_all = jnp.concatenate(
            [slab_ref[j, b * pitch:b * pitch + tt_len, :] for j in range(2 * nc)], axis=1)
        y = jnp.dot(h_all.astype(BF16), cb, preferred_element_type=F32) + d_ref[...] * u_ref[b]
        y_ref[b] = jax.nn.gelu(y)

    @pl.when(tt == pl.num_programs(1) - 1)
    def _():
        hr_out[...] = jnp.concatenate([h_last[j] for j in range(nc)], axis=1)
        hi_out[...] = jnp.concatenate([h_last[nc + j] for j in range(nc)], axis=1)


def _s5_params(lam_re, lam_im, log_dt, b_re, b_im, c_re, c_im, d_skip):
    g, n = lam_re.shape
    nblk = g // GROUPS_PER_BLOCK
    dt = jnp.exp(log_dt)[:, None]
    mag = jnp.exp(lam_re * dt)
    ab_re = mag * jnp.cos(lam_im * dt)
    ab_im = mag * jnp.sin(lam_im * dt)
    w_re = ab_re - 1.0
    w_im = ab_im
    den = lam_re * lam_re + lam_im * lam_im
    coef_re = (w_re * lam_re + w_im * lam_im) / den
    coef_im = (w_im * lam_re - w_re * lam_im) / den
    bb_re = coef_re[..., None] * b_re - coef_im[..., None] * b_im
    bb_im = coef_re[..., None] * b_im + coef_im[..., None] * b_re
    eye = jnp.eye(GROUPS_PER_BLOCK, dtype=F32)

    def in_block(bb):
        bb = bb.reshape(nblk, GROUPS_PER_BLOCK, n, SSM_GROUP)
        blk = bb.transpose(0, 1, 3, 2)[:, :, :, None, :] * eye[None, :, None, :, None]
        return blk.reshape(nblk, LANES, STATES_PER_BLOCK)

    def out_block(c):
        c = c.reshape(nblk, GROUPS_PER_BLOCK, SSM_GROUP, n)
        blk = c.transpose(0, 1, 3, 2)[:, :, :, None, :] * eye[None, :, None, :, None]
        return blk.reshape(nblk, STATES_PER_BLOCK, LANES)

    bblk = jnp.concatenate([in_block(bb_re), in_block(bb_im)], axis=2).astype(BF16)
    cblk = jnp.concatenate([out_block(c_re), out_block(-c_im)], axis=1).astype(BF16)
    return (ab_re.reshape(1, g * n), ab_im.reshape(1, g * n), bblk, cblk,
            d_skip.reshape(1, g * SSM_GROUP))


def _s5(u, h0_re, h0_im, prm, time_tile):
    ab_re, ab_im, bblk, cblk, dsk = prm
    nb, t_len, width = u.shape
    nblk = width // LANES
    tt_len = min(time_tile, t_len)
    pitch = tt_len + SUBLANES if tt_len > SUBLANES else tt_len
    spb = STATES_PER_BLOCK
    state_spec = pl.BlockSpec((nb, spb), lambda c, t: (0, c))
    lam_spec = pl.BlockSpec((1, spb), lambda c, t: (0, c))
    u_spec = pl.BlockSpec((nb, tt_len, LANES), lambda c, t: (0, t, c))
    return pl.pallas_call(
        functools.partial(_s5_kernel, pitch=pitch),
        grid=(nblk, t_len // tt_len),
        in_specs=[u_spec, state_spec, state_spec, lam_spec, lam_spec,
                  pl.BlockSpec((None, LANES, 2 * spb), lambda c, t: (c, 0, 0)),
                  pl.BlockSpec((None, 2 * spb, LANES), lambda c, t: (c, 0, 0)),
                  pl.BlockSpec((1, LANES), lambda c, t: (0, c))],
        out_specs=[u_spec, state_spec, state_spec],
        out_shape=[jax.ShapeDtypeStruct((nb, t_len, width), F32),
                   jax.ShapeDtypeStruct(h0_re.shape, F32),
                   jax.ShapeDtypeStruct(h0_im.shape, F32)],
        scratch_shapes=[pltpu.VMEM((2 * STATE_COLS, nb * pitch, LANES), F32),
                        pltpu.VMEM((2 * STATE_COLS, nb, LANES), F32)],
        compiler_params=_params("arbitrary", "arbitrary"),
        name="s5_mixer",
    )(u, h0_re, h0_im, ab_re, ab_im, bblk, cblk, dsk)


def _log_keep_and_log_sigmoid(z):
    soft = jnp.log1p(jnp.exp(-jnp.abs(z)))
    return -(jnp.maximum(z, 0.0) + soft), jnp.minimum(z, 0.0) - soft


def _split_bf16(x):
    hi = x.astype(BF16)
    return hi, (x - hi.astype(F32)).astype(BF16)


def _attn_prompt_kernel(bias_ref, q_ref, k_ref, v_ref, o_ref, *, blk):
    qi = pl.program_id(2)
    bias = bias_ref[pl.program_id(1)]
    scale = HEAD_DIM ** -0.5
    q = q_ref[...].astype(BF16)
    row = lax.broadcasted_iota(jnp.int32, (blk, blk), 0)
    col = lax.broadcasted_iota(jnp.int32, (blk, blk), 1)
    suffix_and_total = jnp.concatenate(
        [(row >= col).astype(BF16), jnp.ones((blk, blk), BF16)], axis=1)
    causal = col < row

    def key_block(kj, carry, acc, masked):
        start = pl.multiple_of(kj * blk, blk)
        kb = k_ref[pl.ds(start, blk), :].astype(BF16)
        vb = v_ref[pl.ds(start, blk), :].astype(BF16)
        z = lax.dot_general(q, kb, (((1,), (1,)), ((), ())),
                            preferred_element_type=F32) * scale + bias
        log_keep, log_sig = _log_keep_and_log_sigmoid(z)
        if masked:
            log_keep = jnp.where(causal, log_keep, 0.0)
        hi, lo = _split_bf16(log_keep)
        sums = (jnp.dot(hi, suffix_and_total, preferred_element_type=F32)
                + jnp.dot(lo, suffix_and_total, preferred_element_type=F32))
        log_after = sums[:, :blk] - log_keep + carry
        w = jnp.exp(log_sig + log_after)
        if masked:
            w = jnp.where(causal, w, 0.0)
        acc = acc + jnp.dot(w.astype(BF16), vb, preferred_element_type=F32)
        return carry + sums[:, blk:], acc

    carry, acc = key_block(qi, jnp.zeros((blk, blk), F32),
                           jnp.zeros((blk, HEAD_DIM), F32), True)

    def body(i, state):
        return key_block(qi - 1 - i, state[0], state[1], False)

    carry, acc = lax.fori_loop(0, qi, body, (carry, acc))
    o_ref[...] = acc


def _attn_prompt(q, k, v, sb_bias, blk=128):
    nb, t_len, width = q.shape
    nh = width // HEAD_DIM
    kv_spec = pl.BlockSpec((None, t_len, HEAD_DIM), lambda b, h, i: (b, 0, h))
    q_spec = pl.BlockSpec((None, blk, HEAD_DIM), lambda b, h, i: (b, i, h))
    return pl.pallas_call(
        functools.partial(_attn_prompt_kernel, blk=blk),
        grid=(nb, nh, t_len // blk),
        in_specs=[pl.BlockSpec(memory_space=pltpu.SMEM), q_spec, kv_spec, kv_spec],
        out_specs=q_spec,
        out_shape=jax.ShapeDtypeStruct((nb, t_len, width), F32),
        compiler_params=_params("arbitrary", "arbitrary", "arbitrary"),
        name="attn_prompt",
    )(sb_bias, q, k, v)


def _attn_sample_kernel(pt_ref, qbd_ref, bias_ref, kn_ref, vn_ref, kc_ref, vc_ref, o_ref,
                        acc_ref, carry_ref, *, n_heads, n_q):
    del pt_ref
    p = pl.program_id(1)
    scale = HEAD_DIM ** -0.5
    row = lax.broadcasted_iota(jnp.int32, (PAGE_SIZE, LANES), 0)
    col = lax.broadcasted_iota(jnp.int32, (PAGE_SIZE, LANES), 1)
    suffix = (col >= row).astype(BF16)
    new_key_mask = row < (col & (n_q - 1))

    def tile(kt, vt, masked):
        z = jnp.dot(kt, qbd_ref[...], preferred_element_type=F32) * scale + bias_ref[...]
        log_keep, log_sig = _log_keep_and_log_sigmoid(z)
        if masked:
            log_keep = jnp.where(new_key_mask, log_keep, 0.0)
        hi, lo = _split_bf16(log_keep)
        sums = (jnp.dot(suffix, hi, preferred_element_type=F32)
                + jnp.dot(suffix, lo, preferred_element_type=F32))
        log_after = sums - log_keep + carry_ref[...]
        w = jnp.exp(log_sig + log_after)
        if masked:
            w = jnp.where(new_key_mask, w, 0.0)
        carry_ref[...] += sums[0:1, :]
        wv = jnp.dot(w.T.astype(BF16), vt, preferred_element_type=F32)
        for h in range(n_heads):
            acc_ref[h] += wv[h * n_q:(h + 1) * n_q, h * HEAD_DIM:(h + 1) * HEAD_DIM]

    @pl.when(p == 0)
    def _():
        acc_ref[...] = jnp.zeros_like(acc_ref)
        carry_ref[...] = jnp.zeros_like(carry_ref)
        pad = jnp.zeros((PAGE_SIZE - n_q, kn_ref.shape[-1]), F32)
        tile(jnp.concatenate([kn_ref[...], pad], axis=0).astype(BF16),
             jnp.concatenate([vn_ref[...], pad], axis=0).astype(BF16), True)

    def by_head(page_ref):
        return jnp.concatenate(
            [page_ref[pl.ds(h, PAGE_SIZE, stride=n_heads), :] for h in range(n_heads)],
            axis=1).astype(BF16)

    tile(by_head(kc_ref), by_head(vc_ref), False)

    @pl.when(p == pl.num_programs(1) - 1)
    def _():
        o_ref[...] = jnp.concatenate([acc_ref[h] for h in range(n_heads)], axis=1)


def _attn_sample(q, k_new, v_new, cache_k, cache_v, page_table, sb_bias, layer):
    nb, n_q, width = q.shape
    nh = width // HEAD_DIM
    n_pages = page_table.shape[1]
    assert nh * n_q == LANES and n_q <= SUBLANES and n_q & (n_q - 1) == 0
    eye = jnp.eye(nh, dtype=F32)
    q4 = q.reshape(nb, n_q, nh, HEAD_DIM).transpose(0, 2, 3, 1)
    qbd = (q4[:, :, :, None, :] * eye[None, :, None, :, None]).reshape(nb, width, LANES)
    bias_row = jnp.repeat(sb_bias, n_q).reshape(1, LANES)
    new_spec = pl.BlockSpec((None, n_q, width), lambda b, p, pt: (b, 0, 0))
    page_spec = pl.BlockSpec((None, None, PAGE_SIZE * nh, HEAD_DIM),
                             lambda b, p, pt: (layer, pt[b, n_pages - 1 - p], 0, 0))
    grid_spec = pltpu.PrefetchScalarGridSpec(
        num_scalar_prefetch=1,
        grid=(nb, n_pages),
        in_specs=[pl.BlockSpec((None, width, LANES), lambda b, p, pt: (b, 0, 0)),
                  pl.BlockSpec((1, LANES), lambda b, p, pt: (0, 0)),
                  new_spec, new_spec, page_spec, page_spec],
        out_specs=new_spec,
        scratch_shapes=[pltpu.VMEM((nh, n_q, HEAD_DIM), F32), pltpu.VMEM((1, LANES), F32)])
    return pl.pallas_call(
        functools.partial(_attn_sample_kernel, n_heads=nh, n_q=n_q),
        grid_spec=grid_spec,
        out_shape=jax.ShapeDtypeStruct((nb, n_q, width), F32),
        compiler_params=_params("arbitrary", "arbitrary"),
        name="attn_sample",
    )(page_table, qbd.astype(BF16), bias_row, k_new, v_new, cache_k, cache_v)


def _layer(x, h0_re, h0_im, attend, w, s5_prm, time_tile):
    nb, t_len, d = x.shape
    rows = nb * t_len
    ssm_w = w["w_glu"].shape[0]
    att_w = (w["w_in"].shape[1] - ssm_w) // 3
    x2 = x.reshape(rows, d)
    hn = _rmsnorm(x2, w["norm_mix_g"], BF16)
    u = _matmul(hn, w["w_in"], col_start=0, n_out=ssm_w)
    q = _matmul(hn, w["w_in"], col_start=ssm_w, n_out=att_w)
    k = _matmul(hn, w["w_in"], col_start=ssm_w + att_w, n_out=att_w)
    v = _matmul(hn, w["w_in"], col_start=ssm_w + 2 * att_w, n_out=att_w)
    y, h_re, h_im = _s5(u.reshape(nb, t_len, ssm_w), h0_re, h0_im, s5_prm, time_tile)
    y2 = y.reshape(rows, ssm_w)
    ssm_out = _matmul(y2, w["w_glu"], epilogue="glu", extras=(y2, w["b_glu"].reshape(1, ssm_w)))
    att_out = attend(q.reshape(nb, t_len, att_w), k.reshape(nb, t_len, att_w),
                     v.reshape(nb, t_len, att_w)).reshape(rows, att_w)
    mixed = _rmsnorm_pair(ssm_out, w["g_ssm"], att_out, w["g_att"])
    x2 = _matmul(mixed, w["w_out"], epilogue="residual", extras=(x2,))
    hf = _rmsnorm(x2, w["norm_ffn_g"], BF16)
    ff = _matmul(hf, w["w_ff1"], epilogue="relu2", out_dtype=BF16)
    x2 = _matmul(ff, w["w_ff2"], epilogue="residual", extras=(x2,), tn=1024, tk=1024)
    return x2.reshape(nb, t_len, d), k, v, h_re, h_im


def kernel(x_prompt, x_sample, cache_k, cache_v, state_ssm_re, state_ssm_im, page_table, norm_mix_g, w_in, lam_re, lam_im, log_dt, b_re, b_im, c_re, c_im, d_skip, w_glu, b_glu, sb_bias, g_ssm, g_att, w_out, norm_ffn_g, w_ff1, w_ff2, final_g):
    depth = w_in.shape[0]
    bp, tp, d = x_prompt.shape
    bs, ts, _ = x_sample.shape
    n_groups, n_state = lam_re.shape[1:]
    nh = sb_bias.shape[1]
    n_phys, page = cache_k.shape[1:3]
    cache_k = cache_k.reshape(depth, n_phys, page * nh, HEAD_DIM)
    cache_v = cache_v.reshape(depth, n_phys, page * nh, HEAD_DIM)
    h_zero = jnp.zeros((bp, n_groups * n_state), F32)
    yp, ys = x_prompt, x_sample
    outs = [[] for _ in range(8)]
    for l in range(depth):
        w = {"norm_mix_g": norm_mix_g[l], "w_in": w_in[l], "w_glu": w_glu[l], "b_glu": b_glu[l],
             "g_ssm": g_ssm[l], "g_att": g_att[l], "w_out": w_out[l],
             "norm_ffn_g": norm_ffn_g[l], "w_ff1": w_ff1[l], "w_ff2": w_ff2[l]}
        s5_prm = _s5_params(lam_re[l], lam_im[l], log_dt[l], b_re[l], b_im[l], c_re[l], c_im[l],
                            d_skip[l])
        bias = sb_bias[l]
        yp, kp, vp, hpr, hpi = _layer(
            yp, h_zero, h_zero, lambda q, k, v: _attn_prompt(q, k, v, bias), w, s5_prm, 512)
        ys, ksm, vsm, hsr, hsi = _layer(
            ys, state_ssm_re[l].reshape(bs, -1), state_ssm_im[l].reshape(bs, -1),
            lambda q, k, v: _attn_sample(q, k, v, cache_k, cache_v, page_table, bias, l),
            w, s5_prm, ts)
        for lst, val in zip(outs, (kp.reshape(bp, tp, nh, HEAD_DIM), vp.reshape(bp, tp, nh, HEAD_DIM),
                                   hpr.reshape(bp, n_groups, n_state), hpi.reshape(bp, n_groups, n_state),
                                   ksm.reshape(bs, ts, nh, HEAD_DIM), vsm.reshape(bs, ts, nh, HEAD_DIM),
                                   hsr.reshape(bs, n_groups, n_state), hsi.reshape(bs, n_groups, n_state))):
            lst.append(val)
    y_prompt = _rmsnorm(yp.reshape(bp * tp, d), final_g, F32).reshape(bp, tp, d)
    y_sample = _rmsnorm(ys.reshape(bs * ts, d), final_g, F32).reshape(bs, ts, d)
    return (y_prompt, y_sample) + tuple(jnp.stack(o) for o in outs)
```

```python
import functools

import jax
import jax.numpy as jnp
from jax import lax
from jax.experimental import pallas as pl
from jax.experimental.pallas import tpu as pltpu

F32 = jnp.float32
BF16 = jnp.bfloat16

LANES = 128
SUBLANES = 8
VMEM_LIMIT = 56 * 1024 * 1024

RMS_EPS = 1e-6
HEAD_DIM = 128
SSM_GROUP = 16
SSM_STATE = 64
PAGE_SIZE = 128
GROUPS_PER_BLOCK = LANES // SSM_GROUP
STATES_PER_BLOCK = GROUPS_PER_BLOCK * SSM_STATE
STATE_COLS = STATES_PER_BLOCK // LANES

WEIGHT_TILE_N = 512
CAST_BLOCK_ELEMS = 1 << 20
ROW_TILE = 1024
FF2_K_TILE = 1024
ATTN_BLOCK = 512
PAGES_PER_STEP = 4
S5_TIME_TILE = 512


def _params(*sem):
    return pltpu.CompilerParams(dimension_semantics=sem, vmem_limit_bytes=VMEM_LIMIT)


def _rms_scale(x):
    return x * lax.rsqrt(jnp.mean(x * x, axis=-1, keepdims=True) + RMS_EPS)


def _rmsnorm_kernel(x_ref, g_ref, o_ref):
    o_ref[...] = (_rms_scale(x_ref[...]) * g_ref[...]).astype(o_ref.dtype)


def _rmsnorm(x, g, out_dtype, rows_per_step=256):
    rows, d = x.shape
    tr = min(rows_per_step, rows)
    return pl.pallas_call(
        _rmsnorm_kernel,
        grid=(rows // tr,),
        in_specs=[pl.BlockSpec((tr, d), lambda i: (i, 0)),
                  pl.BlockSpec((1, d), lambda i: (0, 0))],
        out_specs=pl.BlockSpec((tr, d), lambda i: (i, 0)),
        out_shape=jax.ShapeDtypeStruct((rows, d), out_dtype),
        compiler_params=_params("arbitrary"),
        name="rmsnorm",
    )(x, g.reshape(1, d))


def _rmsnorm_pair_kernel(a_ref, ga_ref, b_ref, gb_ref, o_ref):
    wa = a_ref.shape[-1]
    o_ref[:, :wa] = (_rms_scale(a_ref[...]) * ga_ref[...]).astype(o_ref.dtype)
    o_ref[:, wa:] = (_rms_scale(b_ref[...]) * gb_ref[...]).astype(o_ref.dtype)


def _rmsnorm_pair(a, ga, b, gb, rows_per_step=256):
    rows, wa = a.shape
    wb = b.shape[1]
    tr = min(rows_per_step, rows)
    return pl.pallas_call(
        _rmsnorm_pair_kernel,
        grid=(rows // tr,),
        in_specs=[pl.BlockSpec((tr, wa), lambda i: (i, 0)),
                  pl.BlockSpec((1, wa), lambda i: (0, 0)),
                  pl.BlockSpec((tr, wb), lambda i: (i, 0)),
                  pl.BlockSpec((1, wb), lambda i: (0, 0))],
        out_specs=pl.BlockSpec((tr, wa + wb), lambda i: (i, 0)),
        out_shape=jax.ShapeDtypeStruct((rows, wa + wb), BF16),
        compiler_params=_params("arbitrary"),
        name="rmsnorm_pair",
    )(a, ga.reshape(1, wa), b, gb.reshape(1, wb))


def _cast_kernel(w_ref, o_ref):
    o_ref[...] = w_ref[...].astype(o_ref.dtype)


def _tile_weight(w, tn=WEIGHT_TILE_N):
    depth, kdim, n = w.shape
    tn = min(tn, n)
    rows = min(kdim, CAST_BLOCK_ELEMS // tn)
    assert n % tn == 0 and kdim % rows == 0
    return pl.pallas_call(
        _cast_kernel,
        grid=(depth, n // tn, kdim // rows),
        in_specs=[pl.BlockSpec((None, rows, tn), lambda l, j, k: (l, k, j))],
        out_specs=pl.BlockSpec((None, None, rows, tn), lambda l, j, k: (l, j, k, 0)),
        out_shape=jax.ShapeDtypeStruct((depth, n // tn, kdim, tn), BF16),
        compiler_params=_params("arbitrary", "arbitrary", "arbitrary"),
        name="tile_weight",
    )(w)


def _mm_kernel(*refs, nk, epilogue):
    n_extra = {"none": 0, "relu2": 0, "residual": 1, "glu": 2}[epilogue]
    a_ref, w_ref = refs[0], refs[1]
    extra = refs[2:2 + n_extra]
    o_ref = refs[2 + n_extra]

    def finish(acc):
        if epilogue == "residual":
            acc = extra[0][...] + acc
        elif epilogue == "relu2":
            acc = jnp.square(jnp.maximum(acc, 0.0))
        elif epilogue == "glu":
            acc = extra[0][...] * jax.nn.sigmoid(acc + extra[1][...])
        o_ref[...] = acc.astype(o_ref.dtype)

    def product():
        return jnp.dot(a_ref[...].astype(BF16), w_ref[...], preferred_element_type=F32)

    if nk == 1:
        finish(product())
        return
    acc_ref = refs[3 + n_extra]
    k = pl.program_id(2)

    @pl.when(k == 0)
    def _():
        acc_ref[...] = jnp.zeros_like(acc_ref)

    acc_ref[...] += product()

    @pl.when(k == nk - 1)
    def _():
        finish(acc_ref[...])


def _matmul(a, wt, layer, *, epilogue="none", extras=(), out_dtype=F32, tm=ROW_TILE, tk=None):
    m, kdim = a.shape
    _, n_tiles, wk, tn = wt.shape
    assert wk == kdim
    tm = min(tm, m)
    tk = kdim if tk is None else min(tk, kdim)
    assert m % tm == 0 and kdim % tk == 0
    nk = kdim // tk
    in_specs = [pl.BlockSpec((tm, tk), lambda i, j, k: (i, k)),
                pl.BlockSpec((None, None, tk, tn), lambda i, j, k: (layer, j, k, 0))]
    for e in extras:
        if e.shape[0] == 1:
            in_specs.append(pl.BlockSpec((1, tn), lambda i, j, k: (0, j)))
        else:
            in_specs.append(pl.BlockSpec((tm, tn), lambda i, j, k: (i, j)))
    scratch = [pltpu.VMEM((tm, tn), F32)] if nk > 1 else []
    return pl.pallas_call(
        functools.partial(_mm_kernel, nk=nk, epilogue=epilogue),
        grid=(m // tm, n_tiles, nk),
        in_specs=in_specs,
        out_specs=pl.BlockSpec((tm, tn), lambda i, j, k: (i, j)),
        out_shape=jax.ShapeDtypeStruct((m, n_tiles * tn), out_dtype),
        scratch_shapes=scratch,
        compiler_params=_params("arbitrary", "arbitrary", "arbitrary"),
        name="matmul_" + epilogue,
    )(a, wt, *extras)


def _s5_kernel(u_ref, h0r_ref, h0i_ref, ar_ref, ai_ref, bblk_ref, cblk_ref, d_ref,
               y_ref, hr_out, hi_out, slab_ref, h_ref, *, pitch):
    nb, tt_len, _ = u_ref.shape
    nc = STATE_COLS
    tt = pl.program_id(1)

    @pl.when(tt == 0)
    def _():
        for j in range(nc):
            h_ref[j] = h0r_ref[:, j * LANES:(j + 1) * LANES]
            h_ref[nc + j] = h0i_ref[:, j * LANES:(j + 1) * LANES]

    bb = bblk_ref[...]
    for b in range(nb):
        x = jnp.dot(u_ref[b].astype(BF16), bb, preferred_element_type=F32)
        for j in range(2 * nc):
            slab_ref[j, b * pitch:b * pitch + tt_len, :] = x[:, j * LANES:(j + 1) * LANES]

    a_re = [jnp.broadcast_to(ar_ref[:, j * LANES:(j + 1) * LANES], (nb, LANES)) for j in range(nc)]
    a_im = [jnp.broadcast_to(ai_ref[:, j * LANES:(j + 1) * LANES], (nb, LANES)) for j in range(nc)]

    def step(t, carry):
        new = []
        for j in range(nc):
            h_re, h_im = carry[j], carry[nc + j]
            rows = pl.ds(t, nb, stride=pitch)
            re_slab, im_slab = slab_ref.at[j], slab_ref.at[nc + j]
            n_re = a_re[j] * h_re - a_im[j] * h_im + re_slab[rows, :]
            n_im = a_re[j] * h_im + a_im[j] * h_re + im_slab[rows, :]
            re_slab[rows, :] = n_re
            im_slab[rows, :] = n_im
            new.append((n_re, n_im))
        return tuple(p[0] for p in new) + tuple(p[1] for p in new)

    h_last = lax.fori_loop(0, tt_len, step, tuple(h_ref[j] for j in range(2 * nc)),
                           unroll=min(4, tt_len))
    for j in range(2 * nc):
        h_ref[j] = h_last[j]

    cb = cblk_ref[...]
    for b in range(nb):
        h_all = jnp.concatenate(
            [slab_ref[j, b * pitch:b * pitch + tt_len, :] for j in range(2 * nc)], axis=1)
        y = jnp.dot(h_all.astype(BF16), cb, preferred_element_type=F32) + d_ref[...] * u_ref[b]
        y_ref[b] = jax.nn.gelu(y)

    @pl.when(tt == pl.num_programs(1) - 1)
    def _():
        hr_out[...] = jnp.concatenate([h_last[j] for j in range(nc)], axis=1)
        hi_out[...] = jnp.concatenate([h_last[nc + j] for j in range(nc)], axis=1)


def _s5_params(lam_re, lam_im, log_dt, b_re, b_im, c_re, c_im, d_skip):
    g, n = lam_re.shape
    nblk = g // GROUPS_PER_BLOCK
    dt = jnp.exp(log_dt)[:, None]
    mag = jnp.exp(lam_re * dt)
    ab_re = mag * jnp.cos(lam_im * dt)
    ab_im = mag * jnp.sin(lam_im * dt)
    w_re = ab_re - 1.0
    w_im = ab_im
    den = lam_re * lam_re + lam_im * lam_im
    coef_re = (w_re * lam_re + w_im * lam_im) / den
    coef_im = (w_im * lam_re - w_re * lam_im) / den
    bb_re = coef_re[..., None] * b_re - coef_im[..., None] * b_im
    bb_im = coef_re[..., None] * b_im + coef_im[..., None] * b_re
    eye = jnp.eye(GROUPS_PER_BLOCK, dtype=F32)

    def in_block(bb):
        bb = bb.reshape(nblk, GROUPS_PER_BLOCK, n, SSM_GROUP)
        blk = bb.transpose(0, 1, 3, 2)[:, :, :, None, :] * eye[None, :, None, :, None]
        return blk.reshape(nblk, LANES, STATES_PER_BLOCK)

    def out_block(c):
        c = c.reshape(nblk, GROUPS_PER_BLOCK, SSM_GROUP, n)
        blk = c.transpose(0, 1, 3, 2)[:, :, :, None, :] * eye[None, :, None, :, None]
        return blk.reshape(nblk, STATES_PER_BLOCK, LANES)

    bblk = jnp.concatenate([in_block(bb_re), in_block(bb_im)], axis=2).astype(BF16)
    cblk = jnp.concatenate([out_block(c_re), out_block(-c_im)], axis=1).astype(BF16)
    return (ab_re.reshape(1, g * n), ab_im.reshape(1, g * n), bblk, cblk,
            d_skip.reshape(1, g * SSM_GROUP))


def _s5(proj, width, h0_re, h0_im, prm, time_tile):
    ab_re, ab_im, bblk, cblk, dsk = prm
    nb, t_len, _ = proj.shape
    nblk = width // LANES
    tt_len = min(time_tile, t_len)
    pitch = tt_len + SUBLANES if tt_len > SUBLANES else tt_len
    spb = STATES_PER_BLOCK
    state_spec = pl.BlockSpec((nb, spb), lambda c, t: (0, c))
    lam_spec = pl.BlockSpec((1, spb), lambda c, t: (0, c))
    u_spec = pl.BlockSpec((nb, tt_len, LANES), lambda c, t: (0, t, c))
    return pl.pallas_call(
        functools.partial(_s5_kernel, pitch=pitch),
        grid=(nblk, t_len // tt_len),
        in_specs=[u_spec, state_spec, state_spec, lam_spec, lam_spec,
                  pl.BlockSpec((None, LANES, 2 * spb), lambda c, t: (c, 0, 0)),
                  pl.BlockSpec((None, 2 * spb, LANES), lambda c, t: (c, 0, 0)),
                  pl.BlockSpec((1, LANES), lambda c, t: (0, c))],
        out_specs=[u_spec, state_spec, state_spec],
        out_shape=[jax.ShapeDtypeStruct((nb, t_len, width), F32),
                   jax.ShapeDtypeStruct(h0_re.shape, F32),
                   jax.ShapeDtypeStruct(h0_im.shape, F32)],
        scratch_shapes=[pltpu.VMEM((2 * STATE_COLS, nb * pitch, LANES), F32),
                        pltpu.VMEM((2 * STATE_COLS, nb, LANES), F32)],
        compiler_params=_params("arbitrary", "arbitrary"),
        name="s5_mixer",
    )(proj, h0_re, h0_im, ab_re, ab_im, bblk, cblk, dsk)


def _log_keep(z):
    return -(jnp.maximum(z, 0.0) + jnp.log(1.0 + jnp.exp(-jnp.abs(z))))


def _split_bf16(x, axis):
    hi = x.astype(BF16)
    return jnp.concatenate([hi, (x - hi.astype(F32)).astype(BF16)], axis=axis)


def _attn_prompt_kernel(bias_ref, sums_ref, q_ref, k_ref, v_ref, o_ref, *, blk):
    qi = pl.program_id(2)
    bias = bias_ref[pl.program_id(1)]
    reps = blk // LANES
    q = (q_ref[...] * HEAD_DIM ** -0.5).astype(BF16)
    row = lax.broadcasted_iota(jnp.int32, (blk, blk), 0)
    col = lax.broadcasted_iota(jnp.int32, (blk, blk), 1)
    causal = col < row

    def key_block(kj, carry, acc, masked):
        start = pl.multiple_of(kj * blk, blk)
        kb = k_ref[pl.ds(start, blk), :].astype(BF16)
        vb = v_ref[pl.ds(start, blk), :].astype(BF16)
        z = lax.dot_general(q, kb, (((1,), (1,)), ((), ())),
                            preferred_element_type=F32) + bias
        log_keep = _log_keep(z)
        if masked:
            log_keep = jnp.where(causal, log_keep, 0.0)
        from_here = [None] * reps
        for c in reversed(range(reps)):
            cols = slice(c * LANES, (c + 1) * LANES)
            sums = jnp.dot(_split_bf16(log_keep[:, cols], 1), sums_ref[...],
                           preferred_element_type=F32)
            from_here[c] = sums[:, :LANES] + carry
            carry = carry + sums[:, LANES:]
        w = jnp.exp(z + jnp.concatenate(from_here, axis=1))
        if masked:
            w = jnp.where(causal, w, 0.0)
        acc = acc + jnp.dot(w.astype(BF16), vb, preferred_element_type=F32)
        return carry, acc

    carry, acc = key_block(qi, jnp.zeros((blk, LANES), F32),
                           jnp.zeros((blk, HEAD_DIM), F32), True)

    def body(i, state):
        return key_block(qi - 1 - i, state[0], state[1], False)

    carry, acc = lax.fori_loop(0, qi, body, (carry, acc))
    o_ref[...] = acc


def _attn_prompt(proj, q_col, k_col, v_col, width, sb_bias, blk=ATTN_BLOCK):
    nb, t_len, _ = proj.shape
    nh = width // HEAD_DIM
    blk = min(blk, t_len)
    qc, kc, vc = q_col // HEAD_DIM, k_col // HEAD_DIM, v_col // HEAD_DIM
    r = lax.broadcasted_iota(jnp.int32, (2 * LANES, 2 * LANES), 0) % LANES
    c = lax.broadcasted_iota(jnp.int32, (2 * LANES, 2 * LANES), 1)
    sums_mat = ((r >= c) | (c >= LANES)).astype(BF16)
    return pl.pallas_call(
        functools.partial(_attn_prompt_kernel, blk=blk),
        grid=(nb, nh, t_len // blk),
        in_specs=[pl.BlockSpec(memory_space=pltpu.SMEM),
                  pl.BlockSpec((2 * LANES, 2 * LANES), lambda b, h, i: (0, 0)),
                  pl.BlockSpec((None, blk, HEAD_DIM), lambda b, h, i: (b, i, qc + h)),
                  pl.BlockSpec((None, t_len, HEAD_DIM), lambda b, h, i: (b, 0, kc + h)),
                  pl.BlockSpec((None, t_len, HEAD_DIM), lambda b, h, i: (b, 0, vc + h))],
        out_specs=pl.BlockSpec((None, blk, HEAD_DIM), lambda b, h, i: (b, i, h)),
        out_shape=jax.ShapeDtypeStruct((nb, t_len, width), F32),
        compiler_params=_params("arbitrary", "arbitrary", "arbitrary"),
        name="attn_prompt",
    )(sb_bias, sums_mat, proj, proj, proj)


def _attn_sample_kernel(pt_ref, qbd_ref, bias_ref, kn_ref, vn_ref, *rest, n_heads, n_q, n_pg):
    del pt_ref
    kc_refs, vc_refs = rest[:n_pg], rest[n_pg:2 * n_pg]
    o_ref, acc_ref, carry_ref = rest[2 * n_pg:]
    p = pl.program_id(1)
    scale = HEAD_DIM ** -0.5
    row = lax.broadcasted_iota(jnp.int32, (PAGE_SIZE, LANES), 0)
    col = lax.broadcasted_iota(jnp.int32, (PAGE_SIZE, LANES), 1)
    suffix = (col >= row).astype(BF16)
    suffix = jnp.concatenate([suffix, suffix], axis=1)
    new_key_mask = row < (col & (n_q - 1))

    def scores(kt, mask):
        z = jnp.dot(kt, qbd_ref[...], preferred_element_type=F32) * scale + bias_ref[...]
        log_keep = _log_keep(z)
        if mask is not None:
            log_keep = jnp.where(mask, log_keep, 0.0)
        sums = jnp.dot(suffix, _split_bf16(log_keep, 0), preferred_element_type=F32)
        return z + sums, sums[0:1, :]

    def accumulate(weights, values):
        wv = jnp.dot(jnp.concatenate([w.T.astype(BF16) for w in weights], axis=1),
                     jnp.concatenate(values, axis=0), preferred_element_type=F32)
        for h in range(n_heads):
            acc_ref[h] += wv[h * n_q:(h + 1) * n_q, h * HEAD_DIM:(h + 1) * HEAD_DIM]

    def by_head(page_ref):
        return jnp.concatenate(
            [page_ref[pl.ds(h, PAGE_SIZE, stride=n_heads), :] for h in range(n_heads)],
            axis=1).astype(BF16)

    @pl.when(p == 0)
    def _():
        acc_ref[...] = jnp.zeros_like(acc_ref)
        pad = jnp.zeros((PAGE_SIZE - n_q, kn_ref.shape[-1]), F32)
        log_w, total = scores(jnp.concatenate([kn_ref[...], pad], axis=0).astype(BF16),
                              new_key_mask)
        carry_ref[...] = total
        accumulate([jnp.where(new_key_mask, jnp.exp(log_w), 0.0)],
                   [jnp.concatenate([vn_ref[...], pad], axis=0).astype(BF16)])

    carry = carry_ref[...]
    weights = []
    for kc_ref in kc_refs:
        log_w, total = scores(by_head(kc_ref), None)
        weights.append(jnp.exp(log_w + carry))
        carry = carry + total
    carry_ref[...] = carry
    accumulate(weights, [by_head(vc_ref) for vc_ref in vc_refs])

    @pl.when(p == pl.num_programs(1) - 1)
    def _():
        o_ref[...] = jnp.concatenate([acc_ref[h] for h in range(n_heads)], axis=1)


def _attn_sample(proj, q_col, k_col, v_col, width, cache_k, cache_v, page_table, sb_bias, layer):
    nb, n_q, _ = proj.shape
    nh = width // HEAD_DIM
    n_pages = page_table.shape[1]
    n_pg = min(PAGES_PER_STEP, n_pages)
    assert nh * n_q == LANES and n_q <= SUBLANES and n_q & (n_q - 1) == 0
    assert n_pages % n_pg == 0 and k_col % width == 0 and v_col % width == 0
    eye = jnp.eye(nh, dtype=F32)
    q4 = proj[:, :, q_col:q_col + width].reshape(nb, n_q, nh, HEAD_DIM).transpose(0, 2, 3, 1)
    qbd = (q4[:, :, :, None, :] * eye[None, :, None, :, None]).reshape(nb, width, LANES)
    bias_row = jnp.repeat(sb_bias, n_q).reshape(1, LANES)
    kb, vb = k_col // width, v_col // width

    def page_spec(j):
        return pl.BlockSpec(
            (None, None, PAGE_SIZE * nh, HEAD_DIM),
            lambda b, p, pt: (layer, pt[b, n_pages - 1 - (p * n_pg + j)], 0, 0))

    pages = [page_spec(j) for j in range(n_pg)]
    grid_spec = pltpu.PrefetchScalarGridSpec(
        num_scalar_prefetch=1,
        grid=(nb, n_pages // n_pg),
        in_specs=[pl.BlockSpec((None, width, LANES), lambda b, p, pt: (b, 0, 0)),
                  pl.BlockSpec((1, LANES), lambda b, p, pt: (0, 0)),
                  pl.BlockSpec((None, n_q, width), lambda b, p, pt: (b, 0, kb)),
                  pl.BlockSpec((None, n_q, width), lambda b, p, pt: (b, 0, vb))] + pages + pages,
        out_specs=pl.BlockSpec((None, n_q, width), lambda b, p, pt: (b, 0, 0)),
        scratch_shapes=[pltpu.VMEM((nh, n_q, HEAD_DIM), F32), pltpu.VMEM((1, LANES), F32)])
    return pl.pallas_call(
        functools.partial(_attn_sample_kernel, n_heads=nh, n_q=n_q, n_pg=n_pg),
        grid_spec=grid_spec,
        out_shape=jax.ShapeDtypeStruct((nb, n_q, width), F32),
        compiler_params=_params("arbitrary", "arbitrary"),
        name="attn_sample",
    )(page_table, qbd.astype(BF16), bias_row, proj, proj,
      *([cache_k] * n_pg), *([cache_v] * n_pg))


def _layer(x, h0_re, h0_im, attend, w, layer, s5_prm, time_tile):
    nb, t_len, d = x.shape
    rows = nb * t_len
    ssm_w = w["b_glu"].shape[-1]
    att_w = w["g_att"].shape[-1]
    x2 = x.reshape(rows, d)
    hn = _rmsnorm(x2, w["norm_mix_g"][layer], BF16)
    proj = _matmul(hn, w["w_in"], layer).reshape(nb, t_len, ssm_w + 3 * att_w)
    y, h_re, h_im = _s5(proj, ssm_w, h0_re, h0_im, s5_prm, time_tile)
    y2 = y.reshape(rows, ssm_w)
    ssm_out = _matmul(y2, w["w_glu"], layer, epilogue="glu",
                      extras=(y2, w["b_glu"][layer].reshape(1, ssm_w)))
    att_out = attend(proj, ssm_w, ssm_w + att_w, ssm_w + 2 * att_w, att_w).reshape(rows, att_w)
    mixed = _rmsnorm_pair(ssm_out, w["g_ssm"][layer], att_out, w["g_att"][layer])
    x2 = _matmul(mixed, w["w_out"], layer, epilogue="residual", extras=(x2,))
    hf = _rmsnorm(x2, w["norm_ffn_g"][layer], BF16)
    ff = _matmul(hf, w["w_ff1"], layer, epilogue="relu2", out_dtype=BF16)
    x2 = _matmul(ff, w["w_ff2"], layer, epilogue="residual", extras=(x2,), tk=FF2_K_TILE)
    return x2.reshape(nb, t_len, d), proj, h_re, h_im


def kernel(x_prompt, x_sample, cache_k, cache_v, state_ssm_re, state_ssm_im, page_table, norm_mix_g, w_in, lam_re, lam_im, log_dt, b_re, b_im, c_re, c_im, d_skip, w_glu, b_glu, sb_bias, g_ssm, g_att, w_out, norm_ffn_g, w_ff1, w_ff2, final_g):
    depth = w_in.shape[0]
    bp, tp, d = x_prompt.shape
    bs, ts, _ = x_sample.shape
    n_groups, n_state = lam_re.shape[1:]
    nh = sb_bias.shape[1]
    n_phys, page = cache_k.shape[1:3]
    ssm_w = w_glu.shape[1]
    att_w = nh * HEAD_DIM
    cache_k = cache_k.reshape(depth, n_phys, page * nh, HEAD_DIM)
    cache_v = cache_v.reshape(depth, n_phys, page * nh, HEAD_DIM)
    w = {"norm_mix_g": norm_mix_g, "b_glu": b_glu, "g_ssm": g_ssm, "g_att": g_att,
         "norm_ffn_g": norm_ffn_g,
         "w_in": _tile_weight(w_in), "w_glu": _tile_weight(w_glu), "w_out": _tile_weight(w_out),
         "w_ff1": _tile_weight(w_ff1), "w_ff2": _tile_weight(w_ff2, 2 * WEIGHT_TILE_N)}
    h_zero = jnp.zeros((bp, n_groups * n_state), F32)
    yp, ys = x_prompt, x_sample
    outs = [[] for _ in range(8)]
    for l in range(depth):
        s5_prm = _s5_params(lam_re[l], lam_im[l], log_dt[l], b_re[l], b_im[l], c_re[l], c_im[l],
                            d_skip[l])
        bias = sb_bias[l]
        yp, proj_p, hpr, hpi = _layer(
            yp, h_zero, h_zero,
            lambda proj, qc, kc, vc, width: _attn_prompt(proj, qc, kc, vc, width, bias),
            w, l, s5_prm, S5_TIME_TILE)
        ys, proj_s, hsr, hsi = _layer(
            ys, state_ssm_re[l].reshape(bs, -1), state_ssm_im[l].reshape(bs, -1),
            lambda proj, qc, kc, vc, width: _attn_sample(
                proj, qc, kc, vc, width, cache_k, cache_v, page_table, bias, l),
            w, l, s5_prm, ts)
        k0, v0 = ssm_w + att_w, ssm_w + 2 * att_w
        vals = (proj_p[:, :, k0:k0 + att_w].reshape(bp, tp, nh, HEAD_DIM),
                proj_p[:, :, v0:v0 + att_w].reshape(bp, tp, nh, HEAD_DIM),
                hpr.reshape(bp, n_groups, n_state), hpi.reshape(bp, n_groups, n_state),
                proj_s[:, :, k0:k0 + att_w].reshape(bs, ts, nh, HEAD_DIM),
                proj_s[:, :, v0:v0 + att_w].reshape(bs, ts, nh, HEAD_DIM),
                hsr.reshape(bs, n_groups, n_state), hsi.reshape(bs, n_groups, n_state))
        for lst, val in zip(outs, vals):
            lst.append(val)
    y_prompt = _rmsnorm(yp.reshape(bp * tp, d), final_g, F32).reshape(bp, tp, d)
    y_sample = _rmsnorm(ys.reshape(bs * ts, d), final_g, F32).reshape(bs, ts, d)
    return (y_prompt, y_sample) + tuple(jnp.stack(o) for o in outs)
```

```python
import functools

import jax
import jax.numpy as jnp
from jax import lax
from jax.experimental import pallas as pl
from jax.experimental.pallas import tpu as pltpu

F32 = jnp.float32
BF16 = jnp.bfloat16

LANES = 128
SUBLANES = 8
VMEM_LIMIT = 56 * 1024 * 1024

RMS_EPS = 1e-6
HEAD_DIM = 128
SSM_GROUP = 16
SSM_STATE = 64
PAGE_SIZE = 128
GROUPS_PER_BLOCK = LANES // SSM_GROUP
STATES_PER_BLOCK = GROUPS_PER_BLOCK * SSM_STATE
STATE_COLS = STATES_PER_BLOCK // LANES

WEIGHT_TILE_N = 512
ROW_TILE = 1024
FF2_COL_TILE = 1024
FF2_ROW_TILE = 2048
FF2_K_TILE = 512
ATTN_Q_BLOCK = 512
ATTN_KEY_BLOCK = 512
PAGES_PER_STEP = 8
S5_TIME_TILE = 512


def _params(*sem):
    return pltpu.CompilerParams(dimension_semantics=sem, vmem_limit_bytes=VMEM_LIMIT)


def _rms_scale(x):
    return x * lax.rsqrt(jnp.mean(x * x, axis=-1, keepdims=True) + RMS_EPS)


def _rmsnorm_kernel(x_ref, g_ref, o_ref):
    o_ref[...] = (_rms_scale(x_ref[...]) * g_ref[...]).astype(o_ref.dtype)


def _rmsnorm(x, g, out_dtype, rows_per_step=256):
    rows, d = x.shape
    tr = min(rows_per_step, rows)
    return pl.pallas_call(
        _rmsnorm_kernel,
        grid=(rows // tr,),
        in_specs=[pl.BlockSpec((tr, d), lambda i: (i, 0)),
                  pl.BlockSpec((1, d), lambda i: (0, 0))],
        out_specs=pl.BlockSpec((tr, d), lambda i: (i, 0)),
        out_shape=jax.ShapeDtypeStruct((rows, d), out_dtype),
        compiler_params=_params("arbitrary"),
        name="rmsnorm",
    )(x, g.reshape(1, d))


def _rmsnorm_pair_kernel(a_ref, ga_ref, b_ref, gb_ref, o_ref):
    wa = a_ref.shape[-1]
    o_ref[:, :wa] = (_rms_scale(a_ref[...]) * ga_ref[...]).astype(o_ref.dtype)
    o_ref[:, wa:] = (_rms_scale(b_ref[...]) * gb_ref[...]).astype(o_ref.dtype)


def _rmsnorm_pair(a, ga, b, gb, rows_per_step=256):
    rows, wa = a.shape
    wb = b.shape[1]
    tr = min(rows_per_step, rows)
    return pl.pallas_call(
        _rmsnorm_pair_kernel,
        grid=(rows // tr,),
        in_specs=[pl.BlockSpec((tr, wa), lambda i: (i, 0)),
                  pl.BlockSpec((1, wa), lambda i: (0, 0)),
                  pl.BlockSpec((tr, wb), lambda i: (i, 0)),
                  pl.BlockSpec((1, wb), lambda i: (0, 0))],
        out_specs=pl.BlockSpec((tr, wa + wb), lambda i: (i, 0)),
        out_shape=jax.ShapeDtypeStruct((rows, wa + wb), BF16),
        compiler_params=_params("arbitrary"),
        name="rmsnorm_pair",
    )(a, ga.reshape(1, wa), b, gb.reshape(1, wb))


def _mm_kernel(*refs, nk, epilogue, emit_tiles):
    n_extra = {"none": 0, "relu2": 0, "residual": 1, "glu": 2}[epilogue]
    a_ref, w_ref = refs[0], refs[1]
    extra = refs[2:2 + n_extra]
    o_ref = refs[2 + n_extra]
    w = w_ref[...].astype(BF16)
    if emit_tiles:
        refs[3 + n_extra][...] = w

    def finish(acc):
        if epilogue == "residual":
            acc = extra[0][...] + acc
        elif epilogue == "relu2":
            acc = jnp.square(jnp.maximum(acc, 0.0))
        elif epilogue == "glu":
            acc = extra[0][...] * jax.nn.sigmoid(acc + extra[1][...])
        o_ref[...] = acc.astype(o_ref.dtype)

    def product():
        return jnp.dot(a_ref[...].astype(BF16), w, preferred_element_type=F32)

    if nk == 1:
        finish(product())
        return

    @pl.when(pl.program_id(2) == 0)
    def _():
        o_ref[...] = extra[0][...] if epilogue == "residual" else jnp.zeros_like(o_ref)

    o_ref[...] += product()


def _matmul(a, w, layer, *, tn=WEIGHT_TILE_N, emit_tiles=False, epilogue="none", extras=(),
            out_dtype=F32, tm=ROW_TILE, tk=None):
    m, kdim = a.shape
    raw = w.dtype == F32
    if raw:
        n = w.shape[2]
        tn = min(tn, n)
        n_tiles = n // tn
        assert w.shape[1] == kdim and n % tn == 0
    else:
        n_tiles, wk, tn = w.shape
        assert wk == kdim and not emit_tiles
    tm = min(tm, m)
    tk = kdim if tk is None else min(tk, kdim)
    assert m % tm == 0 and kdim % tk == 0
    nk = kdim // tk
    assert nk == 1 or (epilogue in ("none", "residual") and out_dtype == F32)
    assert not emit_tiles or m == tm
    tile_spec = pl.BlockSpec((None, tk, tn), lambda i, j, k: (j, k, 0))
    w_spec = pl.BlockSpec((None, tk, tn), lambda i, j, k: (layer, k, j)) if raw else tile_spec
    in_specs = [pl.BlockSpec((tm, tk), lambda i, j, k: (i, k)), w_spec]
    for e in extras:
        if e.shape[0] == 1:
            in_specs.append(pl.BlockSpec((1, tn), lambda i, j, k: (0, j)))
        else:
            in_specs.append(pl.BlockSpec((tm, tn), lambda i, j, k: (i, j)))
    out_specs = [pl.BlockSpec((tm, tn), lambda i, j, k: (i, j))]
    out_shape = [jax.ShapeDtypeStruct((m, n_tiles * tn), out_dtype)]
    if emit_tiles:
        out_specs.append(tile_spec)
        out_shape.append(jax.ShapeDtypeStruct((n_tiles, kdim, tn), BF16))
    outs = pl.pallas_call(
        functools.partial(_mm_kernel, nk=nk, epilogue=epilogue, emit_tiles=emit_tiles),
        grid=(m // tm, n_tiles, nk),
        in_specs=in_specs,
        out_specs=out_specs,
        out_shape=out_shape,
        compiler_params=_params("arbitrary", "arbitrary", "arbitrary"),
        name="matmul_" + epilogue,
    )(a, w, *extras)
    return (outs[0], outs[1]) if emit_tiles else outs[0]


def _s5_kernel(u_ref, h0r_ref, h0i_ref, ar_ref, ai_ref, bblk_ref, cblk_ref, d_ref,
               y_ref, hr_out, hi_out, slab_ref, h_ref, *, pitch):
    nb, tt_len, _ = u_ref.shape
    nc = STATE_COLS
    tt = pl.program_id(1)

    @pl.when(tt == 0)
    def _():
        for j in range(nc):
            h_ref[j] = h0r_ref[:, j * LANES:(j + 1) * LANES]
            h_ref[nc + j] = h0i_ref[:, j * LANES:(j + 1) * LANES]

    bb = bblk_ref[...]
    for b in range(nb):
        x = jnp.dot(u_ref[b].astype(BF16), bb, preferred_element_type=F32)
        for j in range(2 * nc):
            slab_ref[j, b * pitch:b * pitch + tt_len, :] = x[:, j * LANES:(j + 1) * LANES]

    a_re = [jnp.broadcast_to(ar_ref[:, j * LANES:(j + 1) * LANES], (nb, LANES)) for j in range(nc)]
    a_im = [jnp.broadcast_to(ai_ref[:, j * LANES:(j + 1) * LANES], (nb, LANES)) for j in range(nc)]

    def step(t, carry):
        new = []
        for j in range(nc):
            h_re, h_im = carry[j], carry[nc + j]
            rows = pl.ds(t, nb, stride=pitch)
            re_slab, im_slab = slab_ref.at[j], slab_ref.at[nc + j]
            n_re = a_re[j] * h_re - a_im[j] * h_im + re_slab[rows, :]
            n_im = a_re[j] * h_im + a_im[j] * h_re + im_slab[rows, :]
            re_slab[rows, :] = n_re
            im_slab[rows, :] = n_im
            new.append((n_re, n_im))
        return tuple(p[0] for p in new) + tuple(p[1] for p in new)

    h_last = lax.fori_loop(0, tt_len, step, tuple(h_ref[j] for j in range(2 * nc)),
                           unroll=min(4, tt_len))
    for j in range(2 * nc):
        h_ref[j] = h_last[j]

    cb = cblk_ref[...]
    for b in range(nb):
        h_all = jnp.concatenate(
            [slab_ref[j, b * pitch:b * pitch + tt_len, :] for j in range(2 * nc)], axis=1)
        y = jnp.dot(h_all.astype(BF16), cb, preferred_element_type=F32) + d_ref[...] * u_ref[b]
        y_ref[b] = jax.nn.gelu(y)

    @pl.when(tt == pl.num_programs(1) - 1)
    def _():
        hr_out[...] = jnp.concatenate([h_last[j] for j in range(nc)], axis=1)
        hi_out[...] = jnp.concatenate([h_last[nc + j] for j in range(nc)], axis=1)


def _s5_params(lam_re, lam_im, log_dt, b_re, b_im, c_re, c_im, d_skip):
    g, n = lam_re.shape
    nblk = g // GROUPS_PER_BLOCK
    dt = jnp.exp(log_dt)[:, None]
    mag = jnp.exp(lam_re * dt)
    ab_re = mag * jnp.cos(lam_im * dt)
    ab_im = mag * jnp.sin(lam_im * dt)
    w_re = ab_re - 1.0
    w_im = ab_im
    den = lam_re * lam_re + lam_im * lam_im
    coef_re = (w_re * lam_re + w_im * lam_im) / den
    coef_im = (w_im * lam_re - w_re * lam_im) / den
    bb_re = coef_re[..., None] * b_re - coef_im[..., None] * b_im
    bb_im = coef_re[..., None] * b_im + coef_im[..., None] * b_re
    eye = jnp.eye(GROUPS_PER_BLOCK, dtype=F32)

    def in_block(bb):
        bb = bb.reshape(nblk, GROUPS_PER_BLOCK, n, SSM_GROUP)
        blk = bb.transpose(0, 1, 3, 2)[:, :, :, None, :] * eye[None, :, None, :, None]
        return blk.reshape(nblk, LANES, STATES_PER_BLOCK)

    def out_block(c):
        c = c.reshape(nblk, GROUPS_PER_BLOCK, SSM_GROUP, n)
        blk = c.transpose(0, 1, 3, 2)[:, :, :, None, :] * eye[None, :, None, :, None]
        return blk.reshape(nblk, STATES_PER_BLOCK, LANES)

    bblk = jnp.concatenate([in_block(bb_re), in_block(bb_im)], axis=2).astype(BF16)
    cblk = jnp.concatenate([out_block(c_re), out_block(-c_im)], axis=1).astype(BF16)
    return (ab_re.reshape(1, g * n), ab_im.reshape(1, g * n), bblk, cblk,
            d_skip.reshape(1, g * SSM_GROUP))


def _s5(proj, width, h0_re, h0_im, prm, time_tile):
    ab_re, ab_im, bblk, cblk, dsk = prm
    nb, t_len, _ = proj.shape
    nblk = width // LANES
    tt_len = min(time_tile, t_len)
    pitch = tt_len + SUBLANES if tt_len > SUBLANES else tt_len
    spb = STATES_PER_BLOCK
    state_spec = pl.BlockSpec((nb, spb), lambda c, t: (0, c))
    lam_spec = pl.BlockSpec((1, spb), lambda c, t: (0, c))
    u_spec = pl.BlockSpec((nb, tt_len, LANES), lambda c, t: (0, t, c))
    return pl.pallas_call(
        functools.partial(_s5_kernel, pitch=pitch),
        grid=(nblk, t_len // tt_len),
        in_specs=[u_spec, state_spec, state_spec, lam_spec, lam_spec,
                  pl.BlockSpec((None, LANES, 2 * spb), lambda c, t: (c, 0, 0)),
                  pl.BlockSpec((None, 2 * spb, LANES), lambda c, t: (c, 0, 0)),
                  pl.BlockSpec((1, LANES), lambda c, t: (0, c))],
        out_specs=[u_spec, state_spec, state_spec],
        out_shape=[jax.ShapeDtypeStruct((nb, t_len, width), F32),
                   jax.ShapeDtypeStruct(h0_re.shape, F32),
                   jax.ShapeDtypeStruct(h0_im.shape, F32)],
        scratch_shapes=[pltpu.VMEM((2 * STATE_COLS, nb * pitch, LANES), F32),
                        pltpu.VMEM((2 * STATE_COLS, nb, LANES), F32)],
        compiler_params=_params("arbitrary", "arbitrary"),
        name="s5_mixer",
    )(proj, h0_re, h0_im, ab_re, ab_im, bblk, cblk, dsk)


def _log_keep(z):
    return -(jnp.maximum(z, 0.0) + jnp.log(1.0 + jnp.exp(-jnp.abs(z))))


def _split_bf16(x, axis):
    hi = x.astype(BF16)
    return jnp.concatenate([hi, (x - hi.astype(F32)).astype(BF16)], axis=axis)


def _attn_prompt_kernel(bias_ref, sums_ref, q_ref, k_ref, v_ref, o_ref, *, tq, tkb):
    qi = pl.program_id(2)
    bias = bias_ref[pl.program_id(1)]
    reps = tkb // LANES
    n_diag = tq // tkb
    q = (q_ref[...] * HEAD_DIM ** -0.5).astype(BF16)
    row = lax.broadcasted_iota(jnp.int32, (tq, tkb), 0)
    col = lax.broadcasted_iota(jnp.int32, (tq, tkb), 1)

    def key_block(kj, carry, acc, diag):
        start = pl.multiple_of(kj * tkb, tkb)
        kb = k_ref[pl.ds(start, tkb), :].astype(BF16)
        vb = v_ref[pl.ds(start, tkb), :].astype(BF16)
        z = lax.dot_general(q, kb, (((1,), (1,)), ((), ())),
                            preferred_element_type=F32) + bias
        log_keep = _log_keep(z)
        if diag is not None:
            causal = col + diag * tkb < row
            log_keep = jnp.where(causal, log_keep, 0.0)
        from_here = [None] * reps
        for c in reversed(range(reps)):
            cols = slice(c * LANES, (c + 1) * LANES)
            sums = jnp.dot(_split_bf16(log_keep[:, cols], 1), sums_ref[...],
                           preferred_element_type=F32)
            from_here[c] = sums[:, :LANES] + carry
            carry = carry + sums[:, LANES:]
        w = jnp.exp(z + jnp.concatenate(from_here, axis=1))
        if diag is not None:
            w = jnp.where(causal, w, 0.0)
        acc = acc + jnp.dot(w.astype(BF16), vb, preferred_element_type=F32)
        return carry, acc

    state = (jnp.zeros((tq, LANES), F32), jnp.zeros((tq, HEAD_DIM), F32))
    for d in reversed(range(n_diag)):
        state = key_block(qi * n_diag + d, state[0], state[1], d)

    def body(i, state):
        return key_block(qi * n_diag - 1 - i, state[0], state[1], None)

    _, acc = lax.fori_loop(0, qi * n_diag, body, state)
    o_ref[...] = acc


def _attn_prompt(proj, q_col, k_col, v_col, width, sb_bias, tq=ATTN_Q_BLOCK, tkb=ATTN_KEY_BLOCK):
    nb, t_len, _ = proj.shape
    nh = width // HEAD_DIM
    tq = min(tq, t_len)
    tkb = min(tkb, tq)
    assert t_len % tq == 0 and tq % tkb == 0 and tkb % LANES == 0
    qc, kc, vc = q_col // HEAD_DIM, k_col // HEAD_DIM, v_col // HEAD_DIM
    r = lax.broadcasted_iota(jnp.int32, (2 * LANES, 2 * LANES), 0) % LANES
    c = lax.broadcasted_iota(jnp.int32, (2 * LANES, 2 * LANES), 1)
    sums_mat = ((r >= c) | (c >= LANES)).astype(BF16)
    return pl.pallas_call(
        functools.partial(_attn_prompt_kernel, tq=tq, tkb=tkb),
        grid=(nb, nh, t_len // tq),
        in_specs=[pl.BlockSpec(memory_space=pltpu.SMEM),
                  pl.BlockSpec((2 * LANES, 2 * LANES), lambda b, h, i: (0, 0)),
                  pl.BlockSpec((None, tq, HEAD_DIM), lambda b, h, i: (b, i, qc + h)),
                  pl.BlockSpec((None, t_len, HEAD_DIM), lambda b, h, i: (b, 0, kc + h)),
                  pl.BlockSpec((None, t_len, HEAD_DIM), lambda b, h, i: (b, 0, vc + h))],
        out_specs=pl.BlockSpec((None, tq, HEAD_DIM), lambda b, h, i: (b, i, h)),
        out_shape=jax.ShapeDtypeStruct((nb, t_len, width), F32),
        compiler_params=_params("arbitrary", "arbitrary", "arbitrary"),
        name="attn_prompt",
    )(sb_bias, sums_mat, proj, proj, proj)


def _attn_sample_kernel(pt_ref, qbd_ref, bias_ref, kn_ref, vn_ref, *rest, n_heads, n_q, n_pg):
    del pt_ref
    kc_refs, vc_refs = rest[:n_pg], rest[n_pg:2 * n_pg]
    o_ref, acc_ref, carry_ref = rest[2 * n_pg:]
    p = pl.program_id(1)
    scale = HEAD_DIM ** -0.5
    row = lax.broadcasted_iota(jnp.int32, (PAGE_SIZE, LANES), 0)
    col = lax.broadcasted_iota(jnp.int32, (PAGE_SIZE, LANES), 1)
    suffix = (col >= row).astype(BF16)
    suffix = jnp.concatenate([suffix, suffix], axis=1)
    new_key_mask = row < (col & (n_q - 1))

    def scores(kt, mask):
        z = jnp.dot(kt, qbd_ref[...], preferred_element_type=F32) * scale + bias_ref[...]
        log_keep = _log_keep(z)
        if mask is not None:
            log_keep = jnp.where(mask, log_keep, 0.0)
        sums = jnp.dot(suffix, _split_bf16(log_keep, 0), preferred_element_type=F32)
        return z + sums, sums[0:1, :]

    def accumulate(weights, values):
        wv = jnp.dot(jnp.concatenate([w.T.astype(BF16) for w in weights], axis=1),
                     jnp.concatenate(values, axis=0), preferred_element_type=F32)
        for h in range(n_heads):
            acc_ref[h] += wv[h * n_q:(h + 1) * n_q, h * HEAD_DIM:(h + 1) * HEAD_DIM]

    def by_head(page_ref):
        return jnp.concatenate(
            [page_ref[pl.ds(h, PAGE_SIZE, stride=n_heads), :] for h in range(n_heads)],
            axis=1).astype(BF16)

    @pl.when(p == 0)
    def _():
        acc_ref[...] = jnp.zeros_like(acc_ref)
        pad = jnp.zeros((PAGE_SIZE - n_q, kn_ref.shape[-1]), F32)
        log_w, total = scores(jnp.concatenate([kn_ref[...], pad], axis=0).astype(BF16),
                              new_key_mask)
        carry_ref[...] = total
        accumulate([jnp.where(new_key_mask, jnp.exp(log_w), 0.0)],
                   [jnp.concatenate([vn_ref[...], pad], axis=0).astype(BF16)])

    carry = carry_ref[...]
    weights = []
    for kc_ref in kc_refs:
        log_w, total = scores(by_head(kc_ref), None)
        weights.append(jnp.exp(log_w + carry))
        carry = carry + total
    carry_ref[...] = carry
    accumulate(weights, [by_head(vc_ref) for vc_ref in vc_refs])

    @pl.when(p == pl.num_programs(1) - 1)
    def _():
        o_ref[...] = jnp.concatenate([acc_ref[h] for h in range(n_heads)], axis=1)


def _attn_sample(proj, q_col, k_col, v_col, width, cache_k, cache_v, page_table, sb_bias, layer):
    nb, n_q, _ = proj.shape
    nh = width // HEAD_DIM
    n_pages = page_table.shape[1]
    n_pg = min(PAGES_PER_STEP, n_pages)
    assert nh * n_q == LANES and n_q <= SUBLANES and n_q & (n_q - 1) == 0
    assert n_pages % n_pg == 0 and k_col % width == 0 and v_col % width == 0
    eye = jnp.eye(nh, dtype=F32)
    q4 = proj[:, :, q_col:q_col + width].reshape(nb, n_q, nh, HEAD_DIM).transpose(0, 2, 3, 1)
    qbd = (q4[:, :, :, None, :] * eye[None, :, None, :, None]).reshape(nb, width, LANES)
    bias_row = jnp.repeat(sb_bias, n_q).reshape(1, LANES)
    kb, vb = k_col // width, v_col // width

    def page_spec(j):
        return pl.BlockSpec(
            (None, None, PAGE_SIZE * nh, HEAD_DIM),
            lambda b, p, pt: (layer, pt[b, n_pages - 1 - (p * n_pg + j)], 0, 0))

    pages = [page_spec(j) for j in range(n_pg)]
    grid_spec = pltpu.PrefetchScalarGridSpec(
        num_scalar_prefetch=1,
        grid=(nb, n_pages // n_pg),
        in_specs=[pl.BlockSpec((None, width, LANES), lambda b, p, pt: (b, 0, 0)),
                  pl.BlockSpec((1, LANES), lambda b, p, pt: (0, 0)),
                  pl.BlockSpec((None, n_q, width), lambda b, p, pt: (b, 0, kb)),
                  pl.BlockSpec((None, n_q, width), lambda b, p, pt: (b, 0, vb))] + pages + pages,
        out_specs=pl.BlockSpec((None, n_q, width), lambda b, p, pt: (b, 0, 0)),
        scratch_shapes=[pltpu.VMEM((nh, n_q, HEAD_DIM), F32), pltpu.VMEM((1, LANES), F32)])
    return pl.pallas_call(
        functools.partial(_attn_sample_kernel, n_heads=nh, n_q=n_q, n_pg=n_pg),
        grid_spec=grid_spec,
        out_shape=jax.ShapeDtypeStruct((nb, n_q, width), F32),
        compiler_params=_params("arbitrary", "arbitrary"),
        name="attn_sample",
    )(page_table, qbd.astype(BF16), bias_row, proj, proj,
      *([cache_k] * n_pg), *([cache_v] * n_pg))


def _layer(x, h0_re, h0_im, attend, w, mats, layer, s5_prm, time_tile):
    nb, t_len, d = x.shape
    rows = nb * t_len
    ssm_w = w["b_glu"].shape[-1]
    att_w = w["g_att"].shape[-1]
    emit = mats["w_in"].dtype == F32
    tiles = {}

    def mm(a, name, **kw):
        out = _matmul(a, mats[name], layer, emit_tiles=emit, **kw)
        if emit:
            out, tiles[name] = out
        return out

    x2 = x.reshape(rows, d)
    hn = _rmsnorm(x2, w["norm_mix_g"][layer], BF16)
    proj = mm(hn, "w_in").reshape(nb, t_len, ssm_w + 3 * att_w)
    y, h_re, h_im = _s5(proj, ssm_w, h0_re, h0_im, s5_prm, time_tile)
    y2 = y.reshape(rows, ssm_w)
    ssm_out = mm(y2, "w_glu", epilogue="glu", extras=(y2, w["b_glu"][layer].reshape(1, ssm_w)))
    att_out = attend(proj, ssm_w, ssm_w + att_w, ssm_w + 2 * att_w, att_w).reshape(rows, att_w)
    mixed = _rmsnorm_pair(ssm_out, w["g_ssm"][layer], att_out, w["g_att"][layer])
    x2 = mm(mixed, "w_out", epilogue="residual", extras=(x2,))
    hf = _rmsnorm(x2, w["norm_ffn_g"][layer], BF16)
    ff = mm(hf, "w_ff1", epilogue="relu2", out_dtype=BF16)
    x2 = mm(ff, "w_ff2", epilogue="residual", extras=(x2,), tn=FF2_COL_TILE,
            tm=FF2_ROW_TILE, tk=FF2_K_TILE)
    return x2.reshape(nb, t_len, d), proj, h_re, h_im, tiles


def kernel(x_prompt, x_sample, cache_k, cache_v, state_ssm_re, state_ssm_im, page_table, norm_mix_g, w_in, lam_re, lam_im, log_dt, b_re, b_im, c_re, c_im, d_skip, w_glu, b_glu, sb_bias, g_ssm, g_att, w_out, norm_ffn_g, w_ff1, w_ff2, final_g):
    depth = w_in.shape[0]
    bp, tp, d = x_prompt.shape
    bs, ts, _ = x_sample.shape
    n_groups, n_state = lam_re.shape[1:]
    nh = sb_bias.shape[1]
    n_phys, page = cache_k.shape[1:3]
    ssm_w = w_glu.shape[1]
    att_w = nh * HEAD_DIM
    cache_k = cache_k.reshape(depth, n_phys, page * nh, HEAD_DIM)
    cache_v = cache_v.reshape(depth, n_phys, page * nh, HEAD_DIM)
    w = {"norm_mix_g": norm_mix_g, "b_glu": b_glu, "g_ssm": g_ssm, "g_att": g_att,
         "norm_ffn_g": norm_ffn_g}
    raw = {"w_in": w_in, "w_glu": w_glu, "w_out": w_out, "w_ff1": w_ff1, "w_ff2": w_ff2}
    h_zero = jnp.zeros((bp, n_groups * n_state), F32)
    yp, ys = x_prompt, x_sample
    outs = [[] for _ in range(8)]
    for l in range(depth):
        s5_prm = _s5_params(lam_re[l], lam_im[l], log_dt[l], b_re[l], b_im[l], c_re[l], c_im[l],
                            d_skip[l])
        bias = sb_bias[l]
        ys, proj_s, hsr, hsi, tiles = _layer(
            ys, state_ssm_re[l].reshape(bs, -1), state_ssm_im[l].reshape(bs, -1),
            lambda proj, qc, kc, vc, width: _attn_sample(
                proj, qc, kc, vc, width, cache_k, cache_v, page_table, bias, l),
            w, raw, l, s5_prm, ts)
        yp, proj_p, hpr, hpi, _ = _layer(
            yp, h_zero, h_zero,
            lambda proj, qc, kc, vc, width: _attn_prompt(proj, qc, kc, vc, width, bias),
            w, tiles, l, s5_prm, S5_TIME_TILE)
        k0, v0 = ssm_w + att_w, ssm_w + 2 * att_w
        vals = (proj_p[:, :, k0:k0 + att_w].reshape(bp, tp, nh, HEAD_DIM),
                proj_p[:, :, v0:v0 + att_w].reshape(bp, tp, nh, HEAD_DIM),
                hpr.reshape(bp, n_groups, n_state), hpi.reshape(bp, n_groups, n_state),
                proj_s[:, :, k0:k0 + att_w].reshape(bs, ts, nh, HEAD_DIM),
                proj_s[:, :, v0:v0 + att_w].reshape(bs, ts, nh, HEAD_DIM),
                hsr.reshape(bs, n_groups, n_state), hsi.reshape(bs, n_groups, n_state))
        for lst, val in zip(outs, vals):
            lst.append(val)
    y_prompt = _rmsnorm(yp.reshape(bp * tp, d), final_g, F32).reshape(bp, tp, d)
    y_sample = _rmsnorm(ys.reshape(bs * ts, d), final_g, F32).reshape(bs, ts, d)
    return (y_prompt, y_sample) + tuple(jnp.stack(o) for o in outs)
```

```python
import functools

import jax
import jax.numpy as jnp
from jax import lax
from jax.experimental import pallas as pl
from jax.experimental.pallas import tpu as pltpu

F32 = jnp.float32
BF16 = jnp.bfloat16

LANES = 128
SUBLANES = 8
VMEM_LIMIT = 56 * 1024 * 1024

RMS_EPS = 1e-6
HEAD_DIM = 128
SSM_GROUP = 16
SSM_STATE = 64
PAGE_SIZE = 128
GROUPS_PER_BLOCK = LANES // SSM_GROUP
STATES_PER_BLOCK = GROUPS_PER_BLOCK * SSM_STATE
STATE_COLS = STATES_PER_BLOCK // LANES

WEIGHT_TILE_N = 512
ROW_TILE = 1024
FF2_COL_TILE = 1024
FF2_ROW_TILE = 2048
FF2_K_TILE = 1024
ATTN_BLOCK = 512
DIAG_ROW_GROUP = 256
PAGES_PER_STEP = 8
S5_TIME_TILE = 512


def _params(*sem):
    return pltpu.CompilerParams(dimension_semantics=sem, vmem_limit_bytes=VMEM_LIMIT)


def _rms_scale(x):
    return x * lax.rsqrt(jnp.mean(x * x, axis=-1, keepdims=True) + RMS_EPS)


def _rmsnorm_kernel(x_ref, g_ref, o_ref):
    o_ref[...] = (_rms_scale(x_ref[...]) * g_ref[...]).astype(o_ref.dtype)


def _rmsnorm(x, g, out_dtype, rows_per_step=256):
    rows, d = x.shape
    tr = min(rows_per_step, rows)
    return pl.pallas_call(
        _rmsnorm_kernel,
        grid=(rows // tr,),
        in_specs=[pl.BlockSpec((tr, d), lambda i: (i, 0)),
                  pl.BlockSpec((1, d), lambda i: (0, 0))],
        out_specs=pl.BlockSpec((tr, d), lambda i: (i, 0)),
        out_shape=jax.ShapeDtypeStruct((rows, d), out_dtype),
        compiler_params=_params("arbitrary"),
        name="rmsnorm",
    )(x, g.reshape(1, d))


def _rmsnorm_pair_kernel(a_ref, ga_ref, b_ref, gb_ref, o_ref):
    wa = a_ref.shape[-1]
    o_ref[:, :wa] = (_rms_scale(a_ref[...]) * ga_ref[...]).astype(o_ref.dtype)
    o_ref[:, wa:] = (_rms_scale(b_ref[...]) * gb_ref[...]).astype(o_ref.dtype)


def _rmsnorm_pair(a, ga, b, gb, rows_per_step=256):
    rows, wa = a.shape
    wb = b.shape[1]
    tr = min(rows_per_step, rows)
    return pl.pallas_call(
        _rmsnorm_pair_kernel,
        grid=(rows // tr,),
        in_specs=[pl.BlockSpec((tr, wa), lambda i: (i, 0)),
                  pl.BlockSpec((1, wa), lambda i: (0, 0)),
                  pl.BlockSpec((tr, wb), lambda i: (i, 0)),
                  pl.BlockSpec((1, wb), lambda i: (0, 0))],
        out_specs=pl.BlockSpec((tr, wa + wb), lambda i: (i, 0)),
        out_shape=jax.ShapeDtypeStruct((rows, wa + wb), BF16),
        compiler_params=_params("arbitrary"),
        name="rmsnorm_pair",
    )(a, ga.reshape(1, wa), b, gb.reshape(1, wb))


def _mm_kernel(*refs, nk, epilogue, emit_tiles):
    n_extra = {"none": 0, "relu2": 0, "residual": 1, "glu": 2}[epilogue]
    a_ref, w_ref = refs[0], refs[1]
    extra = refs[2:2 + n_extra]
    o_ref = refs[2 + n_extra]
    w = w_ref[...].astype(BF16)
    if emit_tiles:
        refs[3 + n_extra][...] = w

    def finish(acc):
        if epilogue == "residual":
            acc = extra[0][...] + acc
        elif epilogue == "relu2":
            acc = jnp.square(jnp.maximum(acc, 0.0))
        elif epilogue == "glu":
            acc = extra[0][...] * jax.nn.sigmoid(acc + extra[1][...])
        o_ref[...] = acc.astype(o_ref.dtype)

    def product():
        return jnp.dot(a_ref[...].astype(BF16), w, preferred_element_type=F32)

    if nk == 1:
        finish(product())
        return

    @pl.when(pl.program_id(2) == 0)
    def _():
        o_ref[...] = extra[0][...] if epilogue == "residual" else jnp.zeros_like(o_ref)

    o_ref[...] += product()


def _matmul(a, w, layer, *, tn=WEIGHT_TILE_N, emit_tiles=False, epilogue="none", extras=(),
            out_dtype=F32, tm=ROW_TILE, tk=None):
    m, kdim = a.shape
    raw = w.dtype == F32
    if raw:
        n = w.shape[2]
        tn = min(tn, n)
        n_tiles = n // tn
        assert w.shape[1] == kdim and n % tn == 0
    else:
        n_tiles, wk, tn = w.shape
        assert wk == kdim and not emit_tiles
    tm = min(tm, m)
    tk = kdim if tk is None else min(tk, kdim)
    assert m % tm == 0 and kdim % tk == 0
    nk = kdim // tk
    assert nk == 1 or (epilogue in ("none", "residual") and out_dtype == F32)
    assert not emit_tiles or m == tm
    tile_spec = pl.BlockSpec((None, tk, tn), lambda i, j, k: (j, k, 0))
    w_spec = pl.BlockSpec((None, tk, tn), lambda i, j, k: (layer, k, j)) if raw else tile_spec
    in_specs = [pl.BlockSpec((tm, tk), lambda i, j, k: (i, k)), w_spec]
    for e in extras:
        if e.shape[0] == 1:
            in_specs.append(pl.BlockSpec((1, tn), lambda i, j, k: (0, j)))
        else:
            in_specs.append(pl.BlockSpec((tm, tn), lambda i, j, k: (i, j)))
    out_specs = [pl.BlockSpec((tm, tn), lambda i, j, k: (i, j))]
    out_shape = [jax.ShapeDtypeStruct((m, n_tiles * tn), out_dtype)]
    if emit_tiles:
        out_specs.append(tile_spec)
        out_shape.append(jax.ShapeDtypeStruct((n_tiles, kdim, tn), BF16))
    outs = pl.pallas_call(
        functools.partial(_mm_kernel, nk=nk, epilogue=epilogue, emit_tiles=emit_tiles),
        grid=(m // tm, n_tiles, nk),
        in_specs=in_specs,
        out_specs=out_specs,
        out_shape=out_shape,
        compiler_params=_params("arbitrary", "arbitrary", "arbitrary"),
        name="matmul_" + epilogue,
    )(a, w, *extras)
    return (outs[0], outs[1]) if emit_tiles else outs[0]


def _s5_kernel(u_ref, h0r_ref, h0i_ref, ar_ref, ai_ref, bblk_ref, cblk_ref, d_ref,
               y_ref, hr_out, hi_out, slab_ref, h_ref, *, pitch):
    nb, tt_len, _ = u_ref.shape
    nc = STATE_COLS
    tt = pl.program_id(1)

    @pl.when(tt == 0)
    def _():
        for j in range(nc):
            h_ref[j] = h0r_ref[:, j * LANES:(j + 1) * LANES]
            h_ref[nc + j] = h0i_ref[:, j * LANES:(j + 1) * LANES]

    bb = bblk_ref[...]
    for b in range(nb):
        x = jnp.dot(u_ref[b].astype(BF16), bb, preferred_element_type=F32)
        for j in range(2 * nc):
            slab_ref[j, b * pitch:b * pitch + tt_len, :] = x[:, j * LANES:(j + 1) * LANES]

    a_re = [jnp.broadcast_to(ar_ref[:, j * LANES:(j + 1) * LANES], (nb, LANES)) for j in range(nc)]
    a_im = [jnp.broadcast_to(ai_ref[:, j * LANES:(j + 1) * LANES], (nb, LANES)) for j in range(nc)]

    def step(t, carry):
        new = []
        for j in range(nc):
            h_re, h_im = carry[j], carry[nc + j]
            rows = pl.ds(t, nb, stride=pitch)
            re_slab, im_slab = slab_ref.at[j], slab_ref.at[nc + j]
            n_re = a_re[j] * h_re - a_im[j] * h_im + re_slab[rows, :]
            n_im = a_re[j] * h_im + a_im[j] * h_re + im_slab[rows, :]
            re_slab[rows, :] = n_re
            im_slab[rows, :] = n_im
            new.append((n_re, n_im))
        return tuple(p[0] for p in new) + tuple(p[1] for p in new)

    h_last = lax.fori_loop(0, tt_len, step, tuple(h_ref[j] for j in range(2 * nc)),
                           unroll=min(4, tt_len))
    for j in range(2 * nc):
        h_ref[j] = h_last[j]

    cb = cblk_ref[...]
    for b in range(nb):
        h_all = jnp.concatenate(
            [slab_ref[j, b * pitch:b * pitch + tt_len, :] for j in range(2 * nc)], axis=1)
        y = jnp.dot(h_all.astype(BF16), cb, preferred_element_type=F32) + d_ref[...] * u_ref[b]
        y_ref[b] = jax.nn.gelu(y)

    @pl.when(tt == pl.num_programs(1) - 1)
    def _():
        hr_out[...] = jnp.concatenate([h_last[j] for j in range(nc)], axis=1)
        hi_out[...] = jnp.concatenate([h_last[nc + j] for j in range(nc)], axis=1)


def _s5_params(lam_re, lam_im, log_dt, b_re, b_im, c_re, c_im, d_skip):
    g, n = lam_re.shape
    nblk = g // GROUPS_PER_BLOCK
    dt = jnp.exp(log_dt)[:, None]
    mag = jnp.exp(lam_re * dt)
    ab_re = mag * jnp.cos(lam_im * dt)
    ab_im = mag * jnp.sin(lam_im * dt)
    w_re = ab_re - 1.0
    w_im = ab_im
    den = lam_re * lam_re + lam_im * lam_im
    coef_re = (w_re * lam_re + w_im * lam_im) / den
    coef_im = (w_im * lam_re - w_re * lam_im) / den
    bb_re = coef_re[..., None] * b_re - coef_im[..., None] * b_im
    bb_im = coef_re[..., None] * b_im + coef_im[..., None] * b_re
    eye = jnp.eye(GROUPS_PER_BLOCK, dtype=F32)

    def in_block(bb):
        bb = bb.reshape(nblk, GROUPS_PER_BLOCK, n, SSM_GROUP)
        blk = bb.transpose(0, 1, 3, 2)[:, :, :, None, :] * eye[None, :, None, :, None]
        return blk.reshape(nblk, LANES, STATES_PER_BLOCK)

    def out_block(c):
        c = c.reshape(nblk, GROUPS_PER_BLOCK, SSM_GROUP, n)
        blk = c.transpose(0, 1, 3, 2)[:, :, :, None, :] * eye[None, :, None, :, None]
        return blk.reshape(nblk, STATES_PER_BLOCK, LANES)

    bblk = jnp.concatenate([in_block(bb_re), in_block(bb_im)], axis=2).astype(BF16)
    cblk = jnp.concatenate([out_block(c_re), out_block(-c_im)], axis=1).astype(BF16)
    return (ab_re.reshape(1, g * n), ab_im.reshape(1, g * n), bblk, cblk,
            d_skip.reshape(1, g * SSM_GROUP))


def _s5(proj, width, h0_re, h0_im, prm, time_tile):
    ab_re, ab_im, bblk, cblk, dsk = prm
    nb, t_len, _ = proj.shape
    nblk = width // LANES
    tt_len = min(time_tile, t_len)
    pitch = tt_len + SUBLANES if tt_len > SUBLANES else tt_len
    spb = STATES_PER_BLOCK
    state_spec = pl.BlockSpec((nb, spb), lambda c, t: (0, c))
    lam_spec = pl.BlockSpec((1, spb), lambda c, t: (0, c))
    u_spec = pl.BlockSpec((nb, tt_len, LANES), lambda c, t: (0, t, c))
    return pl.pallas_call(
        functools.partial(_s5_kernel, pitch=pitch),
        grid=(nblk, t_len // tt_len),
        in_specs=[u_spec, state_spec, state_spec, lam_spec, lam_spec,
                  pl.BlockSpec((None, LANES, 2 * spb), lambda c, t: (c, 0, 0)),
                  pl.BlockSpec((None, 2 * spb, LANES), lambda c, t: (c, 0, 0)),
                  pl.BlockSpec((1, LANES), lambda c, t: (0, c))],
        out_specs=[u_spec, state_spec, state_spec],
        out_shape=[jax.ShapeDtypeStruct((nb, t_len, width), F32),
                   jax.ShapeDtypeStruct(h0_re.shape, F32),
                   jax.ShapeDtypeStruct(h0_im.shape, F32)],
        scratch_shapes=[pltpu.VMEM((2 * STATE_COLS, nb * pitch, LANES), F32),
                        pltpu.VMEM((2 * STATE_COLS, nb, LANES), F32)],
        compiler_params=_params("arbitrary", "arbitrary"),
        name="s5_mixer",
    )(proj, h0_re, h0_im, ab_re, ab_im, bblk, cblk, dsk)


def _softplus(z):
    return jnp.maximum(z, 0.0) + jnp.log(1.0 + jnp.exp(-jnp.abs(z)))


def _split_bf16(x, axis):
    hi = x.astype(BF16)
    return jnp.concatenate([hi, (x - hi.astype(F32)).astype(BF16)], axis=axis)


def _attn_prompt_kernel(bias_ref, sums_ref, q_ref, k_ref, v_ref, o_ref, *, blk):
    qi = pl.program_id(2)
    bias = bias_ref[pl.program_id(1)]
    q = (q_ref[...] * HEAD_DIM ** -0.5).astype(BF16)
    group = min(blk, DIAG_ROW_GROUP)

    def sweep(q_rows, kb, vb, dropped, acc, first_row):
        n_sub = kb.shape[0] // LANES
        z = lax.dot_general(q_rows, kb, (((1,), (1,)), ((), ())),
                            preferred_element_type=F32) + bias
        weights = [None] * n_sub
        for c in reversed(range(n_sub)):
            z_c = z[:, c * LANES:(c + 1) * LANES]
            drop = _softplus(z_c)
            masked = first_row is not None and (c + 1) * LANES > first_row
            if masked:
                shape = (q_rows.shape[0], LANES)
                earlier = (lax.broadcasted_iota(jnp.int32, shape, 1) + c * LANES
                           < lax.broadcasted_iota(jnp.int32, shape, 0) + first_row)
                drop = jnp.where(earlier, drop, 0.0)
            sums = jnp.dot(_split_bf16(drop, 1), sums_ref[...], preferred_element_type=F32)
            w = jnp.exp(z_c - (sums[:, :LANES] + dropped))
            weights[c] = jnp.where(earlier, w, 0.0) if masked else w
            dropped = dropped + sums[:, LANES:]
        acc = acc + jnp.dot(jnp.concatenate(weights, axis=1).astype(BF16), vb,
                            preferred_element_type=F32)
        return dropped, acc

    def load(ref, kj):
        return ref[pl.ds(pl.multiple_of(kj * blk, blk), blk), :].astype(BF16)

    kb, vb = load(k_ref, qi), load(v_ref, qi)
    parts = []
    for first_row in range(0, blk, group):
        n_keys = first_row + group
        parts.append(sweep(q[first_row:n_keys], kb[:n_keys], vb[:n_keys],
                           jnp.zeros((group, LANES), F32), jnp.zeros((group, HEAD_DIM), F32),
                           first_row))
    state = (jnp.concatenate([p[0] for p in parts], axis=0),
             jnp.concatenate([p[1] for p in parts], axis=0))

    def body(i, state):
        kj = qi - 1 - i
        return sweep(q, load(k_ref, kj), load(v_ref, kj), state[0], state[1], None)

    _, acc = lax.fori_loop(0, qi, body, state)
    o_ref[...] = acc


def _attn_prompt(proj, q_col, k_col, v_col, width, sb_bias, blk=ATTN_BLOCK):
    nb, t_len, _ = proj.shape
    nh = width // HEAD_DIM
    tq = min(blk, t_len)
    assert t_len % tq == 0 and tq % LANES == 0
    qc, kc, vc = q_col // HEAD_DIM, k_col // HEAD_DIM, v_col // HEAD_DIM
    r = lax.broadcasted_iota(jnp.int32, (2 * LANES, 2 * LANES), 0) % LANES
    c = lax.broadcasted_iota(jnp.int32, (2 * LANES, 2 * LANES), 1)
    sums_mat = ((r >= c) | (c >= LANES)).astype(BF16)
    return pl.pallas_call(
        functools.partial(_attn_prompt_kernel, blk=tq),
        grid=(nb, nh, t_len // tq),
        in_specs=[pl.BlockSpec(memory_space=pltpu.SMEM),
                  pl.BlockSpec((2 * LANES, 2 * LANES), lambda b, h, i: (0, 0)),
                  pl.BlockSpec((None, tq, HEAD_DIM), lambda b, h, i: (b, i, qc + h)),
                  pl.BlockSpec((None, t_len, HEAD_DIM), lambda b, h, i: (b, 0, kc + h)),
                  pl.BlockSpec((None, t_len, HEAD_DIM), lambda b, h, i: (b, 0, vc + h))],
        out_specs=pl.BlockSpec((None, tq, HEAD_DIM), lambda b, h, i: (b, i, h)),
        out_shape=jax.ShapeDtypeStruct((nb, t_len, width), F32),
        compiler_params=_params("arbitrary", "arbitrary", "arbitrary"),
        name="attn_prompt",
    )(sb_bias, sums_mat, proj, proj, proj)


def _attn_sample_kernel(pt_ref, qbd_ref, bias_ref, kn_ref, vn_ref, *rest, n_heads, n_q, n_pg):
    del pt_ref
    kc_refs, vc_refs = rest[:n_pg], rest[n_pg:2 * n_pg]
    o_ref, acc_ref, carry_ref = rest[2 * n_pg:]
    p = pl.program_id(1)
    scale = HEAD_DIM ** -0.5
    row = lax.broadcasted_iota(jnp.int32, (PAGE_SIZE, LANES), 0)
    col = lax.broadcasted_iota(jnp.int32, (PAGE_SIZE, LANES), 1)
    suffix = (col >= row).astype(BF16)
    suffix = jnp.concatenate([suffix, suffix], axis=1)
    new_key_mask = row < (col & (n_q - 1))

    def scores(kt, mask):
        z = jnp.dot(kt, qbd_ref[...], preferred_element_type=F32) * scale + bias_ref[...]
        drop = _softplus(z)
        if mask is not None:
            drop = jnp.where(mask, drop, 0.0)
        sums = jnp.dot(suffix, _split_bf16(drop, 0), preferred_element_type=F32)
        return z - sums, sums[0:1, :]

    def accumulate(weights, values):
        wv = jnp.dot(jnp.concatenate([w.T.astype(BF16) for w in weights], axis=1),
                     jnp.concatenate(values, axis=0), preferred_element_type=F32)
        for h in range(n_heads):
            acc_ref[h] += wv[h * n_q:(h + 1) * n_q, h * HEAD_DIM:(h + 1) * HEAD_DIM]

    def by_head(page_ref):
        return jnp.concatenate(
            [page_ref[pl.ds(h, PAGE_SIZE, stride=n_heads), :] for h in range(n_heads)],
            axis=1).astype(BF16)

    @pl.when(p == 0)
    def _():
        acc_ref[...] = jnp.zeros_like(acc_ref)
        pad = jnp.zeros((PAGE_SIZE - n_q, kn_ref.shape[-1]), F32)
        log_w, total = scores(jnp.concatenate([kn_ref[...], pad], axis=0).astype(BF16),
                              new_key_mask)
        carry_ref[...] = total
        accumulate([jnp.where(new_key_mask, jnp.exp(log_w), 0.0)],
                   [jnp.concatenate([vn_ref[...], pad], axis=0).astype(BF16)])

    carry = carry_ref[...]
    weights = []
    for kc_ref in kc_refs:
        log_w, total = scores(by_head(kc_ref), None)
        weights.append(jnp.exp(log_w - carry))
        carry = carry + total
    carry_ref[...] = carry
    accumulate(weights, [by_head(vc_ref) for vc_ref in vc_refs])

    @pl.when(p == pl.num_programs(1) - 1)
    def _():
        o_ref[...] = jnp.concatenate([acc_ref[h] for h in range(n_heads)], axis=1)


def _attn_sample(proj, q_col, k_col, v_col, width, cache_k, cache_v, page_table, sb_bias, layer):
    nb, n_q, _ = proj.shape
    nh = width // HEAD_DIM
    n_pages = page_table.shape[1]
    n_pg = min(PAGES_PER_STEP, n_pages)
    assert nh * n_q == LANES and n_q <= SUBLANES and n_q & (n_q - 1) == 0
    assert n_pages % n_pg == 0 and k_col % width == 0 and v_col % width == 0
    eye = jnp.eye(nh, dtype=F32)
    q4 = proj[:, :, q_col:q_col + width].reshape(nb, n_q, nh, HEAD_DIM).transpose(0, 2, 3, 1)
    qbd = (q4[:, :, :, None, :] * eye[None, :, None, :, None]).reshape(nb, width, LANES)
    bias_row = jnp.repeat(sb_bias, n_q).reshape(1, LANES)
    kb, vb = k_col // width, v_col // width

    def page_spec(j):
        return pl.BlockSpec(
            (None, None, PAGE_SIZE * nh, HEAD_DIM),
            lambda b, p, pt: (layer, pt[b, n_pages - 1 - (p * n_pg + j)], 0, 0))

    pages = [page_spec(j) for j in range(n_pg)]
    grid_spec = pltpu.PrefetchScalarGridSpec(
        num_scalar_prefetch=1,
        grid=(nb, n_pages // n_pg),
        in_specs=[pl.BlockSpec((None, width, LANES), lambda b, p, pt: (b, 0, 0)),
                  pl.BlockSpec((1, LANES), lambda b, p, pt: (0, 0)),
                  pl.BlockSpec((None, n_q, width), lambda b, p, pt: (b, 0, kb)),
                  pl.BlockSpec((None, n_q, width), lambda b, p, pt: (b, 0, vb))] + pages + pages,
        out_specs=pl.BlockSpec((None, n_q, width), lambda b, p, pt: (b, 0, 0)),
        scratch_shapes=[pltpu.VMEM((nh, n_q, HEAD_DIM), F32), pltpu.VMEM((1, LANES), F32)])
    return pl.pallas_call(
        functools.partial(_attn_sample_kernel, n_heads=nh, n_q=n_q, n_pg=n_pg),
        grid_spec=grid_spec,
        out_shape=jax.ShapeDtypeStruct((nb, n_q, width), F32),
        compiler_params=_params("arbitrary", "arbitrary"),
        name="attn_sample",
    )(page_table, qbd.astype(BF16), bias_row, proj, proj,
      *([cache_k] * n_pg), *([cache_v] * n_pg))


def _layer(x, h0_re, h0_im, attend, w, mats, layer, s5_prm, time_tile):
    nb, t_len, d = x.shape
    rows = nb * t_len
    ssm_w = w["b_glu"].shape[-1]
    att_w = w["g_att"].shape[-1]
    emit = mats["w_in"].dtype == F32
    tiles = {}

    def mm(a, name, **kw):
        out = _matmul(a, mats[name], layer, emit_tiles=emit, **kw)
        if emit:
            out, tiles[name] = out
        return out

    x2 = x.reshape(rows, d)
    hn = _rmsnorm(x2, w["norm_mix_g"][layer], BF16)
    proj = mm(hn, "w_in").reshape(nb, t_len, ssm_w + 3 * att_w)
    y, h_re, h_im = _s5(proj, ssm_w, h0_re, h0_im, s5_prm, time_tile)
    y2 = y.reshape(rows, ssm_w)
    ssm_out = mm(y2, "w_glu", epilogue="glu", extras=(y2, w["b_glu"][layer].reshape(1, ssm_w)))
    att_out = attend(proj, ssm_w, ssm_w + att_w, ssm_w + 2 * att_w, att_w).reshape(rows, att_w)
    mixed = _rmsnorm_pair(ssm_out, w["g_ssm"][layer], att_out, w["g_att"][layer])
    x2 = mm(mixed, "w_out", epilogue="residual", extras=(x2,))
    hf = _rmsnorm(x2, w["norm_ffn_g"][layer], BF16)
    ff = mm(hf, "w_ff1", epilogue="relu2", out_dtype=BF16)
    x2 = mm(ff, "w_ff2", epilogue="residual", extras=(x2,), tn=FF2_COL_TILE,
            tm=FF2_ROW_TILE, tk=FF2_K_TILE)
    return x2.reshape(nb, t_len, d), proj, h_re, h_im, tiles


def kernel(x_prompt, x_sample, cache_k, cache_v, state_ssm_re, state_ssm_im, page_table, norm_mix_g, w_in, lam_re, lam_im, log_dt, b_re, b_im, c_re, c_im, d_skip, w_glu, b_glu, sb_bias, g_ssm, g_att, w_out, norm_ffn_g, w_ff1, w_ff2, final_g):
    depth = w_in.shape[0]
    bp, tp, d = x_prompt.shape
    bs, ts, _ = x_sample.shape
    n_groups, n_state = lam_re.shape[1:]
    nh = sb_bias.shape[1]
    n_phys, page = cache_k.shape[1:3]
    ssm_w = w_glu.shape[1]
    att_w = nh * HEAD_DIM
    cache_k = cache_k.reshape(depth, n_phys, page * nh, HEAD_DIM)
    cache_v = cache_v.reshape(depth, n_phys, page * nh, HEAD_DIM)
    w = {"norm_mix_g": norm_mix_g, "b_glu": b_glu, "g_ssm": g_ssm, "g_att": g_att,
         "norm_ffn_g": norm_ffn_g}
    raw = {"w_in": w_in, "w_glu": w_glu, "w_out": w_out, "w_ff1": w_ff1, "w_ff2": w_ff2}
    h_zero = jnp.zeros((bp, n_groups * n_state), F32)
    yp, ys = x_prompt, x_sample
    outs = [[] for _ in range(8)]
    for l in range(depth):
        s5_prm = _s5_params(lam_re[l], lam_im[l], log_dt[l], b_re[l], b_im[l], c_re[l], c_im[l],
                            d_skip[l])
        bias = sb_bias[l]
        ys, proj_s, hsr, hsi, tiles = _layer(
            ys, state_ssm_re[l].reshape(bs, -1), state_ssm_im[l].reshape(bs, -1),
            lambda proj, qc, kc, vc, width: _attn_sample(
                proj, qc, kc, vc, width, cache_k, cache_v, page_table, bias, l),
            w, raw, l, s5_prm, ts)
        yp, proj_p, hpr, hpi, _ = _layer(
            yp, h_zero, h_zero,
            lambda proj, qc, kc, vc, width: _attn_prompt(proj, qc, kc, vc, width, bias),
            w, tiles, l, s5_prm, S5_TIME_TILE)
        k0, v0 = ssm_w + att_w, ssm_w + 2 * att_w
        vals = (proj_p[:, :, k0:k0 + att_w].reshape(bp, tp, nh, HEAD_DIM),
                proj_p[:, :, v0:v0 + att_w].reshape(bp, tp, nh, HEAD_DIM),
                hpr.reshape(bp, n_groups, n_state), hpi.reshape(bp, n_groups, n_state),
                proj_s[:, :, k0:k0 + att_w].reshape(bs, ts, nh, HEAD_DIM),
                proj_s[:, :, v0:v0 + att_w].reshape(bs, ts, nh, HEAD_DIM),
                hsr.reshape(bs, n_groups, n_state), hsi.reshape(bs, n_groups, n_state))
        for lst, val in zip(outs, vals):
            lst.append(val)
    y_prompt = _rmsnorm(yp.reshape(bp * tp, d), final_g, F32).reshape(bp, tp, d)
    y_sample = _rmsnorm(ys.reshape(bs * ts, d), final_g, F32).reshape(bs, ts, d)
    return (y_prompt, y_sample) + tuple(jnp.stack(o) for o in outs)
```

```python
import functools

import jax
import jax.numpy as jnp
from jax import lax
from jax.experimental import pallas as pl
from jax.experimental.pallas import tpu as pltpu

F32 = jnp.float32
BF16 = jnp.bfloat16

LANES = 128
SUBLANES = 8
VMEM_LIMIT = 56 * 1024 * 1024

RMS_EPS = 1e-6
HEAD_DIM = 128
SSM_GROUP = 16
SSM_STATE = 64
PAGE_SIZE = 128
GROUPS_PER_BLOCK = LANES // SSM_GROUP
STATES_PER_BLOCK = GROUPS_PER_BLOCK * SSM_STATE
STATE_COLS = STATES_PER_BLOCK // LANES

WEIGHT_TILE_N = 512
ROW_TILE = 1024
FF2_COL_TILE = 1024
FF2_ROW_TILE = 2048
FF2_K_TILE = 1024
ATTN_BLOCK = 512
DIAG_ROW_GROUP = 256
PAGES_PER_STEP = 8
S5_TIME_TILE = 512


def _params(*sem):
    return pltpu.CompilerParams(dimension_semantics=sem, vmem_limit_bytes=VMEM_LIMIT)


def _rms_scale(x):
    return x * lax.rsqrt(jnp.mean(x * x, axis=-1, keepdims=True) + RMS_EPS)


def _rmsnorm_kernel(x_ref, g_ref, o_ref):
    o_ref[...] = (_rms_scale(x_ref[...]) * g_ref[...]).astype(o_ref.dtype)


def _rmsnorm(x, g, out_dtype, rows_per_step=256):
    rows, d = x.shape
    tr = min(rows_per_step, rows)
    return pl.pallas_call(
        _rmsnorm_kernel,
        grid=(rows // tr,),
        in_specs=[pl.BlockSpec((tr, d), lambda i: (i, 0)),
                  pl.BlockSpec((1, d), lambda i: (0, 0))],
        out_specs=pl.BlockSpec((tr, d), lambda i: (i, 0)),
        out_shape=jax.ShapeDtypeStruct((rows, d), out_dtype),
        compiler_params=_params("arbitrary"),
        name="rmsnorm",
    )(x, g.reshape(1, d))


def _rmsnorm_pair_kernel(a_ref, ga_ref, b_ref, gb_ref, o_ref):
    wa = a_ref.shape[-1]
    o_ref[:, :wa] = (_rms_scale(a_ref[...]) * ga_ref[...]).astype(o_ref.dtype)
    o_ref[:, wa:] = (_rms_scale(b_ref[...]) * gb_ref[...]).astype(o_ref.dtype)


def _rmsnorm_pair(a, ga, b, gb, rows_per_step=256):
    rows, wa = a.shape
    wb = b.shape[1]
    tr = min(rows_per_step, rows)
    return pl.pallas_call(
        _rmsnorm_pair_kernel,
        grid=(rows // tr,),
        in_specs=[pl.BlockSpec((tr, wa), lambda i: (i, 0)),
                  pl.BlockSpec((1, wa), lambda i: (0, 0)),
                  pl.BlockSpec((tr, wb), lambda i: (i, 0)),
                  pl.BlockSpec((1, wb), lambda i: (0, 0))],
        out_specs=pl.BlockSpec((tr, wa + wb), lambda i: (i, 0)),
        out_shape=jax.ShapeDtypeStruct((rows, wa + wb), BF16),
        compiler_params=_params("arbitrary"),
        name="rmsnorm_pair",
    )(a, ga.reshape(1, wa), b, gb.reshape(1, wb))


def _mm_kernel(*refs, nk, epilogue, emit_tiles, col_groups):
    n_extra = {"none": 0, "relu2": 0, "residual": 1, "glu": 2}[epilogue]
    a_ref, w_ref = refs[0], refs[1]
    extra = refs[2:2 + n_extra]
    o_refs = refs[2 + n_extra:2 + n_extra + len(col_groups)]
    o_ref = o_refs[0]
    w = w_ref[...].astype(BF16)
    if emit_tiles:
        refs[2 + n_extra + len(col_groups)][...] = w

    def finish(acc):
        if epilogue == "residual":
            acc = extra[0][...] + acc
        elif epilogue == "relu2":
            acc = jnp.square(jnp.maximum(acc, 0.0))
        elif epilogue == "glu":
            acc = extra[0][...] * jax.nn.sigmoid(acc + extra[1][...])
        o_ref[...] = acc.astype(o_ref.dtype)

    def product():
        return jnp.dot(a_ref[...].astype(BF16), w, preferred_element_type=F32)

    if len(col_groups) > 1:
        j = pl.program_id(1)
        first = 0
        for ref, width in zip(o_refs, col_groups):
            @pl.when((j >= first) & (j < first + width))
            def _(ref=ref):
                ref[...] = product()
            first += width
        return
    if nk == 1:
        finish(product())
        return

    @pl.when(pl.program_id(2) == 0)
    def _():
        o_ref[...] = extra[0][...] if epilogue == "residual" else jnp.zeros_like(o_ref)

    o_ref[...] += product()


def _matmul(a, w, layer, *, tn=WEIGHT_TILE_N, emit_tiles=False, epilogue="none", extras=(),
            out_dtype=F32, tm=ROW_TILE, tk=None, col_splits=None):
    m, kdim = a.shape
    raw = w.dtype == F32
    if raw:
        n = w.shape[2]
        tn = min(tn, n)
        n_tiles = n // tn
        assert w.shape[1] == kdim and n % tn == 0
    else:
        n_tiles, wk, tn = w.shape
        assert wk == kdim and not emit_tiles
    tm = min(tm, m)
    tk = kdim if tk is None else min(tk, kdim)
    assert m % tm == 0 and kdim % tk == 0
    nk = kdim // tk
    assert nk == 1 or (epilogue in ("none", "residual") and out_dtype == F32)
    assert not emit_tiles or m == tm
    tile_spec = pl.BlockSpec((None, tk, tn), lambda i, j, k: (j, k, 0))
    w_spec = pl.BlockSpec((None, tk, tn), lambda i, j, k: (layer, k, j)) if raw else tile_spec
    in_specs = [pl.BlockSpec((tm, tk), lambda i, j, k: (i, k)), w_spec]
    for e in extras:
        if e.shape[0] == 1:
            in_specs.append(pl.BlockSpec((1, tn), lambda i, j, k: (0, j)))
        else:
            in_specs.append(pl.BlockSpec((tm, tn), lambda i, j, k: (i, j)))
    col_splits = (n_tiles * tn,) if col_splits is None else col_splits
    assert sum(col_splits) == n_tiles * tn and all(c % tn == 0 for c in col_splits)
    assert len(col_splits) == 1 or (nk == 1 and epilogue == "none")
    col_groups = tuple(c // tn for c in col_splits)
    out_specs, out_shape, first = [], [], 0
    for width in col_groups:
        out_specs.append(pl.BlockSpec(
            (tm, tn), lambda i, j, k, first=first, last=width - 1: (i, jnp.clip(j - first, 0, last))))
        out_shape.append(jax.ShapeDtypeStruct((m, width * tn), out_dtype))
        first += width
    if emit_tiles:
        out_specs.append(tile_spec)
        out_shape.append(jax.ShapeDtypeStruct((n_tiles, kdim, tn), BF16))
    outs = pl.pallas_call(
        functools.partial(_mm_kernel, nk=nk, epilogue=epilogue, emit_tiles=emit_tiles,
                          col_groups=col_groups),
        grid=(m // tm, n_tiles, nk),
        in_specs=in_specs,
        out_specs=out_specs,
        out_shape=out_shape,
        compiler_params=_params("arbitrary", "arbitrary", "arbitrary"),
        name="matmul_" + epilogue,
    )(a, w, *extras)
    return outs[0] if len(outs) == 1 else tuple(outs)


def _s5_kernel(u_ref, h0r_ref, h0i_ref, ar_ref, ai_ref, bblk_ref, cblk_ref, d_ref,
               y_ref, hr_out, hi_out, slab_ref, h_ref, *, pitch):
    nb, tt_len, _ = u_ref.shape
    nc = STATE_COLS
    tt = pl.program_id(1)

    @pl.when(tt == 0)
    def _():
        for j in range(nc):
            h_ref[j] = h0r_ref[:, j * LANES:(j + 1) * LANES]
            h_ref[nc + j] = h0i_ref[:, j * LANES:(j + 1) * LANES]

    bb = bblk_ref[...]
    for b in range(nb):
        x = jnp.dot(u_ref[b].astype(BF16), bb, preferred_element_type=F32)
        for j in range(2 * nc):
            slab_ref[j, b * pitch:b * pitch + tt_len, :] = x[:, j * LANES:(j + 1) * LANES]

    a_re = [jnp.broadcast_to(ar_ref[:, j * LANES:(j + 1) * LANES], (nb, LANES)) for j in range(nc)]
    a_im = [jnp.broadcast_to(ai_ref[:, j * LANES:(j + 1) * LANES], (nb, LANES)) for j in range(nc)]

    def step(t, carry):
        new = []
        for j in range(nc):
            h_re, h_im = carry[j], carry[nc + j]
            rows = pl.ds(t, nb, stride=pitch)
            re_slab, im_slab = slab_ref.at[j], slab_ref.at[nc + j]
            n_re = a_re[j] * h_re - a_im[j] * h_im + re_slab[rows, :]
            n_im = a_re[j] * h_im + a_im[j] * h_re + im_slab[rows, :]
            re_slab[rows, :] = n_re
            im_slab[rows, :] = n_im
            new.append((n_re, n_im))
        return tuple(p[0] for p in new) + tuple(p[1] for p in new)

    h_last = lax.fori_loop(0, tt_len, step, tuple(h_ref[j] for j in range(2 * nc)),
                           unroll=min(4, tt_len))
    for j in range(2 * nc):
        h_ref[j] = h_last[j]

    cb = cblk_ref[...]
    for b in range(nb):
        h_all = jnp.concatenate(
            [slab_ref[j, b * pitch:b * pitch + tt_len, :] for j in range(2 * nc)], axis=1)
        y = jnp.dot(h_all.astype(BF16), cb, preferred_element_type=F32) + d_ref[...] * u_ref[b]
        y_ref[b] = jax.nn.gelu(y)

    @pl.when(tt == pl.num_programs(1) - 1)
    def _():
        hr_out[...] = jnp.concatenate([h_last[j] for j in range(nc)], axis=1)
        hi_out[...] = jnp.concatenate([h_last[nc + j] for j in range(nc)], axis=1)


def _s5_params(lam_re, lam_im, log_dt, b_re, b_im, c_re, c_im, d_skip):
    g, n = lam_re.shape
    nblk = g // GROUPS_PER_BLOCK
    dt = jnp.exp(log_dt)[:, None]
    mag = jnp.exp(lam_re * dt)
    ab_re = mag * jnp.cos(lam_im * dt)
    ab_im = mag * jnp.sin(lam_im * dt)
    w_re = ab_re - 1.0
    w_im = ab_im
    den = lam_re * lam_re + lam_im * lam_im
    coef_re = (w_re * lam_re + w_im * lam_im) / den
    coef_im = (w_im * lam_re - w_re * lam_im) / den
    bb_re = coef_re[..., None] * b_re - coef_im[..., None] * b_im
    bb_im = coef_re[..., None] * b_im + coef_im[..., None] * b_re
    eye = jnp.eye(GROUPS_PER_BLOCK, dtype=F32)

    def in_block(bb):
        bb = bb.reshape(nblk, GROUPS_PER_BLOCK, n, SSM_GROUP)
        blk = bb.transpose(0, 1, 3, 2)[:, :, :, None, :] * eye[None, :, None, :, None]
        return blk.reshape(nblk, LANES, STATES_PER_BLOCK)

    def out_block(c):
        c = c.reshape(nblk, GROUPS_PER_BLOCK, SSM_GROUP, n)
        blk = c.transpose(0, 1, 3, 2)[:, :, :, None, :] * eye[None, :, None, :, None]
        return blk.reshape(nblk, STATES_PER_BLOCK, LANES)

    bblk = jnp.concatenate([in_block(bb_re), in_block(bb_im)], axis=2).astype(BF16)
    cblk = jnp.concatenate([out_block(c_re), out_block(-c_im)], axis=1).astype(BF16)
    return (ab_re.reshape(1, g * n), ab_im.reshape(1, g * n), bblk, cblk,
            d_skip.reshape(1, g * SSM_GROUP))


def _s5(proj, width, h0_re, h0_im, prm, time_tile):
    ab_re, ab_im, bblk, cblk, dsk = prm
    nb, t_len, _ = proj.shape
    nblk = width // LANES
    tt_len = min(time_tile, t_len)
    pitch = tt_len + SUBLANES if tt_len > SUBLANES else tt_len
    spb = STATES_PER_BLOCK
    state_spec = pl.BlockSpec((nb, spb), lambda c, t: (0, c))
    lam_spec = pl.BlockSpec((1, spb), lambda c, t: (0, c))
    u_spec = pl.BlockSpec((nb, tt_len, LANES), lambda c, t: (0, t, c))
    return pl.pallas_call(
        functools.partial(_s5_kernel, pitch=pitch),
        grid=(nblk, t_len // tt_len),
        in_specs=[u_spec, state_spec, state_spec, lam_spec, lam_spec,
                  pl.BlockSpec((None, LANES, 2 * spb), lambda c, t: (c, 0, 0)),
                  pl.BlockSpec((None, 2 * spb, LANES), lambda c, t: (c, 0, 0)),
                  pl.BlockSpec((1, LANES), lambda c, t: (0, c))],
        out_specs=[u_spec, state_spec, state_spec],
        out_shape=[jax.ShapeDtypeStruct((nb, t_len, width), F32),
                   jax.ShapeDtypeStruct(h0_re.shape, F32),
                   jax.ShapeDtypeStruct(h0_im.shape, F32)],
        scratch_shapes=[pltpu.VMEM((2 * STATE_COLS, nb * pitch, LANES), F32),
                        pltpu.VMEM((2 * STATE_COLS, nb, LANES), F32)],
        compiler_params=_params("arbitrary", "arbitrary"),
        name="s5_mixer",
    )(proj, h0_re, h0_im, ab_re, ab_im, bblk, cblk, dsk)


def _softplus(z):
    return jnp.maximum(z, 0.0) + jnp.log(1.0 + jnp.exp(-jnp.abs(z)))


def _split_bf16(x, axis):
    hi = x.astype(BF16)
    return jnp.concatenate([hi, (x - hi.astype(F32)).astype(BF16)], axis=axis)


def _attn_prompt_kernel(bias_ref, sums_ref, q_ref, k_ref, v_ref, o_ref, *, blk):
    qi = pl.program_id(2)
    bias = bias_ref[pl.program_id(1)]
    q = (q_ref[...] * HEAD_DIM ** -0.5).astype(BF16)
    group = min(blk, DIAG_ROW_GROUP)

    def sweep(q_rows, kb, vb, dropped, acc, first_row):
        n_sub = kb.shape[0] // LANES
        z = lax.dot_general(q_rows, kb, (((1,), (1,)), ((), ())),
                            preferred_element_type=F32) + bias
        weights = [None] * n_sub
        for c in reversed(range(n_sub)):
            z_c = z[:, c * LANES:(c + 1) * LANES]
            drop = _softplus(z_c)
            masked = first_row is not None and (c + 1) * LANES > first_row
            if masked:
                shape = (q_rows.shape[0], LANES)
                earlier = (lax.broadcasted_iota(jnp.int32, shape, 1) + c * LANES
                           < lax.broadcasted_iota(jnp.int32, shape, 0) + first_row)
                drop = jnp.where(earlier, drop, 0.0)
            sums = jnp.dot(_split_bf16(drop, 1), sums_ref[...], preferred_element_type=F32)
            w = jnp.exp(z_c - (sums[:, :LANES] + dropped))
            weights[c] = jnp.where(earlier, w, 0.0) if masked else w
            dropped = dropped + sums[:, LANES:]
        acc = acc + jnp.dot(jnp.concatenate(weights, axis=1).astype(BF16), vb,
                            preferred_element_type=F32)
        return dropped, acc

    def load(ref, kj):
        return ref[pl.ds(pl.multiple_of(kj * blk, blk), blk), :].astype(BF16)

    kb, vb = load(k_ref, qi), load(v_ref, qi)
    parts = []
    for first_row in range(0, blk, group):
        n_keys = first_row + group
        parts.append(sweep(q[first_row:n_keys], kb[:n_keys], vb[:n_keys],
                           jnp.zeros((group, LANES), F32), jnp.zeros((group, HEAD_DIM), F32),
                           first_row))
    state = (jnp.concatenate([p[0] for p in parts], axis=0),
             jnp.concatenate([p[1] for p in parts], axis=0))

    def body(i, state):
        kj = qi - 1 - i
        return sweep(q, load(k_ref, kj), load(v_ref, kj), state[0], state[1], None)

    _, acc = lax.fori_loop(0, qi, body, state)
    o_ref[...] = acc


def _attn_prompt(q_arr, q_col, k_arr, v_arr, sb_bias, blk=ATTN_BLOCK):
    nb, t_len, width = k_arr.shape
    nh = width // HEAD_DIM
    tq = min(blk, t_len)
    assert t_len % tq == 0 and tq % LANES == 0
    qc = q_col // HEAD_DIM
    r = lax.broadcasted_iota(jnp.int32, (2 * LANES, 2 * LANES), 0) % LANES
    c = lax.broadcasted_iota(jnp.int32, (2 * LANES, 2 * LANES), 1)
    sums_mat = ((r >= c) | (c >= LANES)).astype(BF16)
    return pl.pallas_call(
        functools.partial(_attn_prompt_kernel, blk=tq),
        grid=(nb, nh, t_len // tq),
        in_specs=[pl.BlockSpec(memory_space=pltpu.SMEM),
                  pl.BlockSpec((2 * LANES, 2 * LANES), lambda b, h, i: (0, 0)),
                  pl.BlockSpec((None, tq, HEAD_DIM), lambda b, h, i: (b, i, qc + h)),
                  pl.BlockSpec((None, t_len, HEAD_DIM), lambda b, h, i: (b, 0, h)),
                  pl.BlockSpec((None, t_len, HEAD_DIM), lambda b, h, i: (b, 0, h))],
        out_specs=pl.BlockSpec((None, tq, HEAD_DIM), lambda b, h, i: (b, i, h)),
        out_shape=jax.ShapeDtypeStruct((nb, t_len, width), F32),
        compiler_params=_params("arbitrary", "arbitrary", "arbitrary"),
        name="attn_prompt",
    )(sb_bias, sums_mat, q_arr, k_arr, v_arr)


def _attn_sample_kernel(pt_ref, qbd_ref, bias_ref, kn_ref, vn_ref, *rest, n_heads, n_q, n_pg):
    del pt_ref
    kc_refs, vc_refs = rest[:n_pg], rest[n_pg:2 * n_pg]
    o_ref, acc_ref, carry_ref = rest[2 * n_pg:]
    p = pl.program_id(1)
    scale = HEAD_DIM ** -0.5
    row = lax.broadcasted_iota(jnp.int32, (PAGE_SIZE, LANES), 0)
    col = lax.broadcasted_iota(jnp.int32, (PAGE_SIZE, LANES), 1)
    suffix = (col >= row).astype(BF16)
    suffix = jnp.concatenate([suffix, suffix], axis=1)
    new_key_mask = row < (col & (n_q - 1))

    def scores(kt, mask):
        z = jnp.dot(kt, qbd_ref[...], preferred_element_type=F32) * scale + bias_ref[...]
        drop = _softplus(z)
        if mask is not None:
            drop = jnp.where(mask, drop, 0.0)
        sums = jnp.dot(suffix, _split_bf16(drop, 0), preferred_element_type=F32)
        return z - sums, sums[0:1, :]

    def accumulate(weights, values):
        wv = jnp.dot(jnp.concatenate([w.T.astype(BF16) for w in weights], axis=1),
                     jnp.concatenate(values, axis=0), preferred_element_type=F32)
        for h in range(n_heads):
            acc_ref[h] += wv[h * n_q:(h + 1) * n_q, h * HEAD_DIM:(h + 1) * HEAD_DIM]

    def by_head(page_ref):
        return jnp.concatenate(
            [page_ref[pl.ds(h, PAGE_SIZE, stride=n_heads), :] for h in range(n_heads)],
            axis=1).astype(BF16)

    @pl.when(p == 0)
    def _():
        acc_ref[...] = jnp.zeros_like(acc_ref)
        pad = jnp.zeros((PAGE_SIZE - n_q, kn_ref.shape[-1]), F32)
        log_w, total = scores(jnp.concatenate([kn_ref[...], pad], axis=0).astype(BF16),
                              new_key_mask)
        carry_ref[...] = total
        accumulate([jnp.where(new_key_mask, jnp.exp(log_w), 0.0)],
                   [jnp.concatenate([vn_ref[...], pad], axis=0).astype(BF16)])

    carry = carry_ref[...]
    weights = []
    for kc_ref in kc_refs:
        log_w, total = scores(by_head(kc_ref), None)
        weights.append(jnp.exp(log_w - carry))
        carry = carry + total
    carry_ref[...] = carry
    accumulate(weights, [by_head(vc_ref) for vc_ref in vc_refs])

    @pl.when(p == pl.num_programs(1) - 1)
    def _():
        o_ref[...] = jnp.concatenate([acc_ref[h] for h in range(n_heads)], axis=1)


def _attn_sample(q_arr, q_col, k_new, v_new, cache_k, cache_v, page_table, sb_bias, layer):
    nb, n_q, width = k_new.shape
    nh = width // HEAD_DIM
    n_pages = page_table.shape[1]
    n_pg = min(PAGES_PER_STEP, n_pages)
    assert nh * n_q == LANES and n_q <= SUBLANES and n_q & (n_q - 1) == 0
    assert n_pages % n_pg == 0
    eye = jnp.eye(nh, dtype=F32)
    q4 = q_arr[:, :, q_col:q_col + width].reshape(nb, n_q, nh, HEAD_DIM).transpose(0, 2, 3, 1)
    qbd = (q4[:, :, :, None, :] * eye[None, :, None, :, None]).reshape(nb, width, LANES)
    bias_row = jnp.repeat(sb_bias, n_q).reshape(1, LANES)

    def page_spec(j):
        return pl.BlockSpec(
            (None, None, PAGE_SIZE * nh, HEAD_DIM),
            lambda b, p, pt: (layer, pt[b, n_pages - 1 - (p * n_pg + j)], 0, 0))

    pages = [page_spec(j) for j in range(n_pg)]
    grid_spec = pltpu.PrefetchScalarGridSpec(
        num_scalar_prefetch=1,
        grid=(nb, n_pages // n_pg),
        in_specs=[pl.BlockSpec((None, width, LANES), lambda b, p, pt: (b, 0, 0)),
                  pl.BlockSpec((1, LANES), lambda b, p, pt: (0, 0)),
                  pl.BlockSpec((None, n_q, width), lambda b, p, pt: (b, 0, 0)),
                  pl.BlockSpec((None, n_q, width), lambda b, p, pt: (b, 0, 0))] + pages + pages,
        out_specs=pl.BlockSpec((None, n_q, width), lambda b, p, pt: (b, 0, 0)),
        scratch_shapes=[pltpu.VMEM((nh, n_q, HEAD_DIM), F32), pltpu.VMEM((1, LANES), F32)])
    return pl.pallas_call(
        functools.partial(_attn_sample_kernel, n_heads=nh, n_q=n_q, n_pg=n_pg),
        grid_spec=grid_spec,
        out_shape=jax.ShapeDtypeStruct((nb, n_q, width), F32),
        compiler_params=_params("arbitrary", "arbitrary"),
        name="attn_sample",
    )(page_table, qbd.astype(BF16), bias_row, k_new, v_new,
      *([cache_k] * n_pg), *([cache_v] * n_pg))


def _layer(x, h0_re, h0_im, attend, w, mats, layer, s5_prm, time_tile):
    nb, t_len, d = x.shape
    rows = nb * t_len
    ssm_w = w["b_glu"].shape[-1]
    att_w = w["g_att"].shape[-1]
    tiles = {}

    def mm(a, name, **kw):
        emit = name in w["emit"] and mats[name].dtype == F32
        out = _matmul(a, mats[name], layer, emit_tiles=emit, **kw)
        if emit:
            *out, tiles[name] = out
            out = out[0] if len(out) == 1 else out
        return out

    x2 = x.reshape(rows, d)
    hn = _rmsnorm(x2, w["norm_mix_g"][layer], BF16)
    uq, k, v = mm(hn, "w_in", col_splits=(ssm_w + att_w, att_w, att_w))
    uq = uq.reshape(nb, t_len, ssm_w + att_w)
    k = k.reshape(nb, t_len, att_w)
    v = v.reshape(nb, t_len, att_w)
    y, h_re, h_im = _s5(uq, ssm_w, h0_re, h0_im, s5_prm, time_tile)
    y2 = y.reshape(rows, ssm_w)
    ssm_out = mm(y2, "w_glu", epilogue="glu", extras=(y2, w["b_glu"][layer].reshape(1, ssm_w)))
    att_out = attend(uq, ssm_w, k, v).reshape(rows, att_w)
    mixed = _rmsnorm_pair(ssm_out, w["g_ssm"][layer], att_out, w["g_att"][layer])
    x2 = mm(mixed, "w_out", epilogue="residual", extras=(x2,))
    hf = _rmsnorm(x2, w["norm_ffn_g"][layer], BF16)
    ff = mm(hf, "w_ff1", epilogue="relu2", out_dtype=BF16)
    x2 = mm(ff, "w_ff2", epilogue="residual", extras=(x2,), tn=FF2_COL_TILE,
            tm=FF2_ROW_TILE, tk=FF2_K_TILE)
    return x2.reshape(nb, t_len, d), k, v, h_re, h_im, tiles


def kernel(x_prompt, x_sample, cache_k, cache_v, state_ssm_re, state_ssm_im, page_table, norm_mix_g, w_in, lam_re, lam_im, log_dt, b_re, b_im, c_re, c_im, d_skip, w_glu, b_glu, sb_bias, g_ssm, g_att, w_out, norm_ffn_g, w_ff1, w_ff2, final_g):
    depth = w_in.shape[0]
    bp, tp, d = x_prompt.shape
    bs, ts, _ = x_sample.shape
    n_groups, n_state = lam_re.shape[1:]
    nh = sb_bias.shape[1]
    n_phys, page = cache_k.shape[1:3]
    cache_k = cache_k.reshape(depth, n_phys, page * nh, HEAD_DIM)
    cache_v = cache_v.reshape(depth, n_phys, page * nh, HEAD_DIM)
    w = {"norm_mix_g": norm_mix_g, "b_glu": b_glu, "g_ssm": g_ssm, "g_att": g_att,
         "norm_ffn_g": norm_ffn_g, "emit": ("w_in", "w_glu", "w_out", "w_ff2")}
    raw = {"w_in": w_in, "w_glu": w_glu, "w_out": w_out, "w_ff1": w_ff1, "w_ff2": w_ff2}
    h_zero = jnp.zeros((bp, n_groups * n_state), F32)
    yp, ys = x_prompt, x_sample
    outs = [[] for _ in range(8)]
    for l in range(depth):
        s5_prm = _s5_params(lam_re[l], lam_im[l], log_dt[l], b_re[l], b_im[l], c_re[l], c_im[l],
                            d_skip[l])
        bias = sb_bias[l]
        ys, ks, vs, hsr, hsi, tiles = _layer(
            ys, state_ssm_re[l].reshape(bs, -1), state_ssm_im[l].reshape(bs, -1),
            lambda q_arr, qc, k, v: _attn_sample(
                q_arr, qc, k, v, cache_k, cache_v, page_table, bias, l),
            w, raw, l, s5_prm, ts)
        yp, kp, vp, hpr, hpi, _ = _layer(
            yp, h_zero, h_zero,
            lambda q_arr, qc, k, v: _attn_prompt(q_arr, qc, k, v, bias),
            w, {**raw, **tiles}, l, s5_prm, S5_TIME_TILE)
        vals = (kp.reshape(bp, tp, nh, HEAD_DIM), vp.reshape(bp, tp, nh, HEAD_DIM),
                hpr.reshape(bp, n_groups, n_state), hpi.reshape(bp, n_groups, n_state),
                ks.reshape(bs, ts, nh, HEAD_DIM), vs.reshape(bs, ts, nh, HEAD_DIM),
                hsr.reshape(bs, n_groups, n_state), hsi.reshape(bs, n_groups, n_state))
        for lst, val in zip(outs, vals):
            lst.append(val)
    y_prompt = _rmsnorm(yp.reshape(bp * tp, d), final_g, F32).reshape(bp, tp, d)
    y_sample = _rmsnorm(ys.reshape(bs * ts, d), final_g, F32).reshape(bs, ts, d)
    return (y_prompt, y_sample) + tuple(jnp.stack(o) for o in outs)
```
